```python
import jax, jax.numpy as jnp
from jax import lax
import numpy as np

D_MODEL = 2048
BATCH = 4
SEQ = 4096
DEPTH = 2

A_HEADS = 16
A_KV_HEADS = 2
A_HEAD_DIM = 64
WINDOW = 128
B_HEADS = 8
B_HEAD_DIM = 128
CONV_K = 4
DN_CHUNK = 64
C_WIDTH = D_MODEL
C_GROUPS = 8
C_CHUNK = 128
C_GROUP_DIM = C_WIDTH // C_GROUPS
D_FF = -(-8 * D_MODEL // (3 * 256)) * 256
EPS = 1e-6

A_Q = A_HEADS * A_HEAD_DIM
A_KV = A_KV_HEADS * A_HEAD_DIM
B_W = B_HEADS * B_HEAD_DIM
EVEN_IN = A_Q + 2 * A_KV + 4 * B_W + 2 * B_HEADS
MIX_OUT = A_Q + B_W
EVEN_SPLITS = [int(s) for s in np.cumsum([A_Q, A_KV, A_KV, 3 * B_W, B_W, B_HEADS])]

kernel_name = "hybrid_swa_sink_gdn_gmlp_block"


def rms_norm(x, g):
    xf = x.astype(jnp.float32)
    y = xf * lax.rsqrt(jnp.mean(xf * xf, axis=-1, keepdims=True) + EPS)
    return (y * g.astype(jnp.float32)).astype(x.dtype)


def layer_norm(x, g, b):
    xf = x.astype(jnp.float32)
    mu = jnp.mean(xf, axis=-1, keepdims=True)
    var = jnp.mean(jnp.square(xf - mu), axis=-1, keepdims=True)
    y = (xf - mu) * lax.rsqrt(var + EPS)
    return (y * g.astype(jnp.float32) + b.astype(jnp.float32)).astype(x.dtype)


def l2_norm(x):
    return x * lax.rsqrt(jnp.sum(x * x, axis=-1, keepdims=True) + EPS)


def sliding_window_attention(q, k, v, sinks):
    b, t, hq, dh = q.shape
    hkv = k.shape[2]
    grp = hq // hkv
    nb = t // WINDOW
    qb = q.astype(jnp.float32).reshape(b, nb, WINDOW, hkv, grp, dh)

    def with_prev(x):
        xb = x.astype(jnp.float32).reshape(b, nb, WINDOW, hkv, dh)
        prev = jnp.pad(xb, ((0, 0), (1, 0), (0, 0), (0, 0), (0, 0)))[:, :-1]
        return jnp.concatenate([prev, xb], axis=2)

    kb, vb = with_prev(k), with_prev(v)
    s = jnp.einsum('bnqhgd,bnkhd->bnhgqk', qb, kb) * (dh ** -0.5)
    r = jnp.arange(WINDOW)[:, None]
    c = jnp.arange(2 * WINDOW)[None, :]
    rel = r + WINDOW - c
    blk = jnp.arange(nb)[:, None, None]
    valid = (rel >= 0) & (rel < WINDOW) & (blk * WINDOW - WINDOW + c >= 0)
    s = jnp.where(valid[None, :, None, None], s, -jnp.inf)
    sink = sinks.astype(jnp.float32).reshape(1, 1, hkv, grp, 1, 1)
    m = jnp.maximum(jnp.max(s, axis=-1, keepdims=True), sink)
    p = jnp.exp(s - m)
    denom = jnp.sum(p, axis=-1, keepdims=True) + jnp.exp(sink - m)
    o = jnp.einsum('bnhgqk,bnkhd->bnqhgd', p / denom, vb)
    return o.reshape(b, t, hq * dh).astype(q.dtype)


def causal_conv_silu(x, w):
    kk, ch = w.shape
    y = lax.conv_general_dilated(
        x, w[:, None, :].astype(x.dtype), window_strides=(1,),
        padding=((kk - 1, 0),), dimension_numbers=('NWC', 'WIO', 'NWC'),
        feature_group_count=ch)
    return jax.nn.silu(y)


def gated_delta_rule(q, k, v, beta, g):
    b, t, h, dk = q.shape
    dv = v.shape[-1]
    c = DN_CHUNK
    n = t // c

    def chunks(x):
        x = jnp.moveaxis(x, 2, 1)
        return x.reshape(b, h, n, c, *x.shape[3:])

    q, k, v, beta, g = (chunks(a) for a in (q, k, v, beta, g))
    gam = jnp.cumsum(g, axis=-1)
    idx = jnp.arange(c)
    incl = idx[:, None] >= idx[None, :]
    strict = idx[:, None] > idx[None, :]
    decay = jnp.exp(jnp.where(incl, gam[..., :, None] - gam[..., None, :], -jnp.inf))
    kk = jnp.einsum('bhnid,bhnjd->bhnij', k, k)
    a_mat = jnp.where(strict, beta[..., :, None] * kk * decay, 0.0) + jnp.eye(c, dtype=q.dtype)
    rhs = jnp.concatenate([v * beta[..., None], k * (beta * jnp.exp(gam))[..., None]], axis=-1)
    sol = lax.linalg.triangular_solve(a_mat, rhs, left_side=True, lower=True, unit_diagonal=True)
    u, w = sol[..., :dv], sol[..., dv:]
    qk = jnp.einsum('bhnid,bhnjd->bhnij', q, k) * decay
    q_dec = q * jnp.exp(gam)[..., None]
    k_dec = k * jnp.exp(gam[..., -1:] - gam)[..., None]
    g_last = jnp.exp(gam[..., -1])

    def step(state, xs):
        qd, kd, wc, uc, qkc, gl = xs
        v_new = uc - jnp.einsum('bhck,bhkv->bhcv', wc, state)
        o = jnp.einsum('bhck,bhkv->bhcv', qd, state) + jnp.einsum('bhij,bhjv->bhiv', qkc, v_new)
        state = state * gl[..., None, None] + jnp.einsum('bhck,bhcv->bhkv', kd, v_new)
        return state, o

    xs = tuple(jnp.moveaxis(a, 2, 0) for a in (q_dec, k_dec, w, u, qk, g_last))
    s0 = jnp.zeros((b, h, dk, dv), q.dtype)
    _, o = lax.scan(step, s0, xs)
    return jnp.transpose(o, (1, 0, 3, 2, 4)).reshape(b, t, h, dv)


def even_mixer(hn, w_in, conv_w, a_log, dt_bias, sinks, onorm, w_out):
    b, t, _ = hn.shape
    proj = hn @ w_in
    qa, ka, va, qkv_b, z, beta_raw, a_raw = jnp.split(proj, EVEN_SPLITS, axis=-1)
    out_a = sliding_window_attention(
        qa.reshape(b, t, A_HEADS, A_HEAD_DIM),
        ka.reshape(b, t, A_KV_HEADS, A_HEAD_DIM),
        va.reshape(b, t, A_KV_HEADS, A_HEAD_DIM), sinks)
    qkv_b = causal_conv_silu(qkv_b, conv_w).astype(jnp.float32)
    qb, kb, vb = jnp.split(qkv_b, 3, axis=-1)
    qb = l2_norm(qb.reshape(b, t, B_HEADS, B_HEAD_DIM)) * (B_HEAD_DIM ** -0.5)
    kb = l2_norm(kb.reshape(b, t, B_HEADS, B_HEAD_DIM))
    vb = vb.reshape(b, t, B_HEADS, B_HEAD_DIM)
    beta = jax.nn.sigmoid(beta_raw.astype(jnp.float32))
    g = -jnp.exp(a_log.astype(jnp.float32)) * jax.nn.softplus(
        a_raw.astype(jnp.float32) + dt_bias.astype(jnp.float32))
    o = gated_delta_rule(qb, kb, vb, beta, g)
    o = o * lax.rsqrt(jnp.mean(o * o, axis=-1, keepdims=True) + EPS) * onorm.astype(jnp.float32)
    o = o * jax.nn.silu(z.astype(jnp.float32).reshape(b, t, B_HEADS, B_HEAD_DIM))
    out_b = o.reshape(b, t, B_W).astype(hn.dtype)
    return jnp.concatenate([out_a, out_b], axis=-1) @ w_out


def odd_mixer(hn, w_in, ln_g, ln_b, w_s, b_s, w_out):
    b, t, _ = hn.shape
    zz = jax.nn.gelu(hn @ w_in, approximate=False)
    u, v = jnp.split(zz, 2, axis=-1)
    v = layer_norm(v, ln_g, ln_b)
    nb = t // C_CHUNK
    vb = v.reshape(b, nb, C_CHUNK, C_GROUPS, C_GROUP_DIM)
    ws = jnp.tril(w_s)
    mixed = jnp.einsum('gts,bnsgc->bntgc', ws, vb) + b_s.T[None, None, :, :, None]
    return (u * mixed.reshape(b, t, C_WIDTH)) @ w_out


def swiglu(hn, w_gate, w_up, w_down):
    return (jax.nn.silu(hn @ w_gate) * (hn @ w_up)) @ w_down


def setup_inputs(seed: int = 0) -> dict:
    key = jax.random.key(seed)
    ks = jax.random.split(key, 22)
    ne, no = (DEPTH + 1) // 2, DEPTH // 2
    f32 = jnp.float32

    def nrm(k, shape, scale):
        return jax.random.normal(k, shape, f32) * scale

    def gain(k, shape):
        return 1.0 + 0.05 * jax.random.normal(k, shape, f32)

    return {
        "x": nrm(ks[0], (BATCH, SEQ, D_MODEL), 1.0),
        "even_norm": gain(ks[1], (ne, D_MODEL)),
        "even_w_in": nrm(ks[2], (ne, D_MODEL, EVEN_IN), D_MODEL ** -0.5),
        "even_conv": nrm(ks[3], (ne, CONV_K, 3 * B_W), CONV_K ** -0.5),
        "even_a_log": jnp.log(jax.random.uniform(ks[4], (ne, B_HEADS), f32, 1.0, 16.0)),
        "even_dt_bias": nrm(ks[5], (ne, B_HEADS), 0.1),
        "even_sinks": nrm(ks[6], (ne, A_HEADS), 1.0),
        "even_onorm": gain(ks[7], (ne, B_HEAD_DIM)),
        "even_w_out": nrm(ks[8], (ne, MIX_OUT, D_MODEL), MIX_OUT ** -0.5),
        "odd_norm": gain(ks[9], (no, D_MODEL)),
        "odd_w_in": nrm(ks[10], (no, D_MODEL, 2 * C_WIDTH), D_MODEL ** -0.5),
        "odd_ln_g": gain(ks[11], (no, C_WIDTH)),
        "odd_ln_b": nrm(ks[12], (no, C_WIDTH), 0.02),
        "odd_w_s": nrm(ks[13], (no, C_GROUPS, C_CHUNK, C_CHUNK), C_CHUNK ** -0.5),
        "odd_b_s": 1.0 + nrm(ks[14], (no, C_GROUPS, C_CHUNK), 0.1),
        "odd_w_out": nrm(ks[15], (no, C_WIDTH, D_MODEL), C_WIDTH ** -0.5),
        "ffn_norm": gain(ks[16], (DEPTH, D_MODEL)),
        "ffn_w_gate": nrm(ks[17], (DEPTH, D_MODEL, D_FF), D_MODEL ** -0.5),
        "ffn_w_up": nrm(ks[18], (DEPTH, D_MODEL, D_FF), D_MODEL ** -0.5),
        "ffn_w_down": nrm(ks[19], (DEPTH, D_FF, D_MODEL), D_FF ** -0.5),
        "final_norm": gain(ks[20], (D_MODEL,)),
    }


def reference(x, even_norm, even_w_in, even_conv, even_a_log, even_dt_bias, even_sinks,
              even_onorm, even_w_out, odd_norm, odd_w_in, odd_ln_g, odd_ln_b, odd_w_s,
              odd_b_s, odd_w_out, ffn_norm, ffn_w_gate, ffn_w_up, ffn_w_down, final_norm):
    h = x
    for i in range(DEPTH):
        j = i // 2
        if i % 2 == 0:
            h = h + even_mixer(rms_norm(h, even_norm[j]), even_w_in[j], even_conv[j],
                               even_a_log[j], even_dt_bias[j], even_sinks[j],
                               even_onorm[j], even_w_out[j])
        else:
            h = h + odd_mixer(rms_norm(h, odd_norm[j]), odd_w_in[j], odd_ln_g[j],
                              odd_ln_b[j], odd_w_s[j], odd_b_s[j], odd_w_out[j])
        h = h + swiglu(rms_norm(h, ffn_norm[i]), ffn_w_gate[i], ffn_w_up[i], ffn_w_down[i])
    return rms_norm(h, final_norm)
```

```python
import functools

import jax
import jax.numpy as jnp
from jax import lax
from jax.experimental import pallas as pl
from jax.experimental.pallas import tpu as pltpu

F32 = jnp.float32
BF16 = jnp.bfloat16

D_MODEL = 2048
A_HEADS = 16
A_KV_HEADS = 2
A_HEAD_DIM = 64
WINDOW = 128
B_HEADS = 8
B_HEAD_DIM = 128
CONV_K = 4
DN_CHUNK = 64
C_GROUPS = 8
C_CHUNK = 128
C_GROUP_DIM = D_MODEL // C_GROUPS
EPS = 1e-6

LANES = 128
SUBLANES = 8
A_Q = A_HEADS * A_HEAD_DIM
A_KV = A_KV_HEADS * A_HEAD_DIM
B_W = B_HEADS * B_HEAD_DIM
EVEN_MAIN = A_Q + 2 * A_KV + 4 * B_W
BLK_KA = A_Q // LANES
BLK_VA = BLK_KA + A_KV // LANES
BLK_QB = BLK_VA + A_KV // LANES
BLK_KB = BLK_QB + B_HEADS
BLK_VB = BLK_KB + B_HEADS
BLK_Z = BLK_VB + B_HEADS

VMEM_LIMIT = 56 * 1024 * 1024


def _params(sem, vmem=VMEM_LIMIT):
    return pltpu.CompilerParams(dimension_semantics=sem, vmem_limit_bytes=vmem)


def _dot(a, b):
    return jnp.dot(a, b, preferred_element_type=F32)


def _dot_nt(a, b):
    return lax.dot_general(a, b, (((1,), (1,)), ((), ())), preferred_element_type=F32)


def _dot_tn(a, b):
    return lax.dot_general(a, b, (((0,), (0,)), ((), ())), preferred_element_type=F32)


def _rms_rows(x, g):
    ms = jnp.mean(x * x, axis=-1, keepdims=True)
    return x * lax.rsqrt(ms + EPS) * g


def _silu(x):
    return x * jax.nn.sigmoid(x)


def _gelu(x):
    return x * (lax.erf(x * (2.0 ** -0.5)) + 1.0) * 0.5


def _norm_proj_kernel(x_ref, g_ref, w_ref, wg_ref, o_ref, og_ref, hn_ref):
    @pl.when(pl.program_id(1) == 0)
    def _():
        hn = _rms_rows(x_ref[...], g_ref[...]).astype(BF16)
        hn_ref[...] = hn
        og_ref[...] = _dot(hn, wg_ref[...])

    o_ref[...] = _dot(hn_ref[...], w_ref[...])


def _norm_proj_even(x, g, w_main, w_gate, *, tm=512, tn=768):
    n, d = x.shape
    nout = w_main.shape[1]
    return pl.pallas_call(
        _norm_proj_kernel,
        out_shape=(jax.ShapeDtypeStruct((n, nout), F32),
                   jax.ShapeDtypeStruct((n, LANES), F32)),
        grid=(n // tm, nout // tn),
        in_specs=[pl.BlockSpec((tm, d), lambda i, j: (i, 0)),
                  pl.BlockSpec((1, d), lambda i, j: (0, 0)),
                  pl.BlockSpec((d, tn), lambda i, j: (0, j)),
                  pl.BlockSpec((d, LANES), lambda i, j: (0, 0))],
        out_specs=(pl.BlockSpec((tm, tn), lambda i, j: (i, j)),
                   pl.BlockSpec((tm, LANES), lambda i, j: (i, 0))),
        scratch_shapes=[pltpu.VMEM((tm, d), BF16)],
        compiler_params=_params(("parallel", "arbitrary")),
        name="norm_proj_even",
    )(x, g, w_main, w_gate)


def _norm_gelu_kernel(x_ref, g_ref, w_ref, o_ref, hn_ref):
    @pl.when(pl.program_id(1) == 0)
    def _():
        hn_ref[...] = _rms_rows(x_ref[...], g_ref[...]).astype(BF16)

    o_ref[...] = _gelu(_dot(hn_ref[...], w_ref[...]))


def _norm_proj_gelu(x, g, w, *, tm=512, tn=512):
    n, d = x.shape
    nout = w.shape[1]
    return pl.pallas_call(
        _norm_gelu_kernel,
        out_shape=jax.ShapeDtypeStruct((n, nout), F32),
        grid=(n // tm, nout // tn),
        in_specs=[pl.BlockSpec((tm, d), lambda i, j: (i, 0)),
                  pl.BlockSpec((1, d), lambda i, j: (0, 0)),
                  pl.BlockSpec((d, tn), lambda i, j: (0, j))],
        out_specs=pl.BlockSpec((tm, tn), lambda i, j: (i, j)),
        scratch_shapes=[pltpu.VMEM((tm, d), BF16)],
        compiler_params=_params(("parallel", "arbitrary")),
        name="norm_proj_gelu",
    )(x, g, w)


def _proj_res_kernel(*refs, n_in):
    a_refs = refs[:n_in]
    w_refs = refs[n_in:2 * n_in]
    res_ref = refs[2 * n_in]
    o_ref = refs[2 * n_in + 1]
    acc = res_ref[...]
    for a_ref, w_ref in zip(a_refs, w_refs):
        acc = acc + _dot(a_ref[...], w_ref[...])
    o_ref[...] = acc


def _proj_residual(a_list, w_list, res, *, tm=512, tn=1024):
    n, d = res.shape
    n_in = len(a_list)
    in_specs = ([pl.BlockSpec((tm, a.shape[1]), lambda i, j: (i, 0)) for a in a_list]
                + [pl.BlockSpec((w.shape[0], tn), lambda i, j: (0, j)) for w in w_list]
                + [pl.BlockSpec((tm, tn), lambda i, j: (i, j))])
    return pl.pallas_call(
        functools.partial(_proj_res_kernel, n_in=n_in),
        out_shape=jax.ShapeDtypeStruct((n, d), F32),
        grid=(n // tm, d // tn),
        in_specs=in_specs,
        out_specs=pl.BlockSpec((tm, tn), lambda i, j: (i, j)),
        compiler_params=_params(("parallel", "arbitrary")),
        name="proj_residual",
    )(*a_list, *w_list, res)


def _ffn_kernel(x_ref, g_ref, wg_ref, wu_ref, wd_ref, fg_ref, o_ref, hn_ref, *, final_norm):
    f = pl.program_id(1)

    @pl.when(f == 0)
    def _():
        x = x_ref[...]
        hn_ref[...] = _rms_rows(x, g_ref[...]).astype(BF16)
        o_ref[...] = x

    hn = hn_ref[...]
    a = (_silu(_dot(hn, wg_ref[...])) * _dot(hn, wu_ref[...])).astype(BF16)
    o_ref[...] += _dot(a, wd_ref[...])

    if final_norm:
        @pl.when(f == pl.num_programs(1) - 1)
        def _():
            o_ref[...] = _rms_rows(o_ref[...], fg_ref[...])


def _ffn(x, g, w_gate, w_up, w_down, final_g, *, final_norm, tm=512, tf=512):
    n, d = x.shape
    dff = w_gate.shape[1]
    return pl.pallas_call(
        functools.partial(_ffn_kernel, final_norm=final_norm),
        out_shape=jax.ShapeDtypeStruct((n, d), F32),
        grid=(n // tm, dff // tf),
        in_specs=[pl.BlockSpec((tm, d), lambda i, f: (i, 0)),
                  pl.BlockSpec((1, d), lambda i, f: (0, 0)),
                  pl.BlockSpec((d, tf), lambda i, f: (0, f)),
                  pl.BlockSpec((d, tf), lambda i, f: (0, f)),
                  pl.BlockSpec((tf, d), lambda i, f: (f, 0)),
                  pl.BlockSpec((1, d), lambda i, f: (0, 0))],
        out_specs=pl.BlockSpec((tm, d), lambda i, f: (i, 0)),
        scratch_shapes=[pltpu.VMEM((tm, d), BF16)],
        compiler_params=_params(("parallel", "arbitrary")),
        name="swiglu_ffn",
    )(x, g, w_gate, w_up, w_down, final_g)


def _swa_kernel(sink_ref, q_ref, k_ref, v_ref, kp_ref, vp_ref, o_ref):
    nblk = pl.program_id(1)
    w = WINDOW
    grp = A_HEADS // A_KV_HEADS
    lane = lax.broadcasted_iota(jnp.int32, (w, LANES), 1)
    low = lane < A_HEAD_DIM
    row = lax.broadcasted_iota(jnp.int32, (w, w), 0)
    col = lax.broadcasted_iota(jnp.int32, (w, w), 1)
    cur_ok = col <= row
    prev_ok = jnp.logical_and(col > row, nblk > 0)
    cur_ok = jnp.concatenate([cur_ok] * grp, axis=0)
    prev_ok = jnp.concatenate([prev_ok] * grp, axis=0)

    def dup_half(x, head):
        swapped = pltpu.roll(x, A_HEAD_DIM, 1)
        if head == 0:
            return jnp.where(low, x, swapped)
        return jnp.where(low, swapped, x)

    kc, vc, kp, vp = k_ref[...], v_ref[...], kp_ref[...], vp_ref[...]
    scale = A_HEAD_DIM ** -0.5
    for kvh in range(A_KV_HEADS):
        k_cur = dup_half(kc, kvh).astype(BF16)
        v_cur = dup_half(vc, kvh).astype(BF16)
        k_prev = dup_half(kp, kvh).astype(BF16)
        v_prev = dup_half(vp, kvh).astype(BF16)
        qs, sinks = [], []
        for hh in range(grp):
            head = kvh * grp + hh
            pair = q_ref[:, (head // 2) * LANES:(head // 2 + 1) * LANES] * scale
            keep = low if head % 2 == 0 else jnp.logical_not(low)
            qs.append(jnp.where(keep, pair, 0.0))
            sinks.append(jnp.full((w, 1), sink_ref[head], F32))
        qst = jnp.concatenate(qs, axis=0).astype(BF16)
        sink = jnp.concatenate(sinks, axis=0)
        s_cur = jnp.where(cur_ok, _dot_nt(qst, k_cur), -jnp.inf)
        s_prev = jnp.where(prev_ok, _dot_nt(qst, k_prev), -jnp.inf)
        m = jnp.maximum(jnp.maximum(jnp.max(s_cur, axis=-1, keepdims=True),
                                    jnp.max(s_prev, axis=-1, keepdims=True)), sink)
        p_cur = jnp.exp(s_cur - m)
        p_prev = jnp.exp(s_prev - m)
        denom = (jnp.sum(p_cur, axis=-1, keepdims=True)
                 + jnp.sum(p_prev, axis=-1, keepdims=True) + jnp.exp(sink - m))
        inv = 1.0 / denom
        o = (_dot((p_cur * inv).astype(BF16), v_cur)
             + _dot((p_prev * inv).astype(BF16), v_prev))
        for pr in range(grp // 2):
            even = o[(2 * pr) * w:(2 * pr + 1) * w, :]
            odd = o[(2 * pr + 1) * w:(2 * pr + 2) * w, :]
            c0 = (kvh * grp // 2 + pr) * LANES
            o_ref[:, c0:c0 + LANES] = jnp.where(low, even, odd).astype(o_ref.dtype)


def _swa(proj, sinks, batch, seq):
    nb = seq // WINDOW
    proj3 = proj.reshape(batch, seq, proj.shape[-1])
    spec = lambda blk, prev: pl.BlockSpec(
        (None, WINDOW, LANES),
        (lambda b, n: (b, jnp.maximum(n - 1, 0), blk)) if prev else (lambda b, n: (b, n, blk)))
    out = pl.pallas_call(
        _swa_kernel,
        out_shape=jax.ShapeDtypeStruct((batch, seq, A_Q), BF16),
        grid=(batch, nb),
        in_specs=[pl.BlockSpec(memory_space=pltpu.SMEM),
                  pl.BlockSpec((None, WINDOW, A_Q), lambda b, n: (b, n, 0)),
                  spec(BLK_KA, False), spec(BLK_VA, False),
                  spec(BLK_KA, True), spec(BLK_VA, True)],
        out_specs=pl.BlockSpec((None, WINDOW, A_Q), lambda b, n: (b, n, 0)),
        compiler_params=_params(("parallel", "arbitrary")),
        name="swa_sink_attention",
    )(sinks, proj3, proj3, proj3, proj3, proj3)
    return out.reshape(batch * seq, A_Q)


GDN_TB = 512
GDN_NC = GDN_TB // DN_CHUNK


def _conv_silu(x, tail, w):
    t = x.shape[0]
    acc = x * w[CONV_K - 1:CONV_K, :]
    x8 = x[0:SUBLANES, :]
    acc8 = x8 * w[CONV_K - 1:CONV_K, :]
    row8 = lax.broadcasted_iota(jnp.int32, (SUBLANES, LANES), 0)
    for j in range(CONV_K - 1):
        s = CONV_K - 1 - j
        wj = w[j:j + 1, :]
        acc = acc + pltpu.roll(x, s, 0) * wj
        head = jnp.where(row8 < s, pltpu.roll(tail, s, 0), pltpu.roll(x8, s, 0))
        acc8 = acc8 + head * wj
    y = jnp.concatenate([acc8, acc[SUBLANES:, :]], axis=0) if t > SUBLANES else acc8
    return _silu(y)


def _split3(x):
    hi = x.astype(BF16)
    r1 = x - hi.astype(F32)
    mid = r1.astype(BF16)
    lo = (r1 - mid.astype(F32)).astype(BF16)
    return hi, mid, lo


def _dot_exact_lhs(t_b, x):
    hi, mid, lo = _split3(x)
    return _dot(t_b, hi) + _dot(t_b, mid) + _dot(t_b, lo)


def _gdn_kernel(q_ref, k_ref, v_ref, z_ref, gate_ref, cq_ref, ck_ref, cv_ref,
                alog_ref, dt_ref, onorm_ref, o_ref, state_ref, tail_ref):
    h = pl.program_id(1)
    c = DN_CHUNK

    @pl.when(pl.program_id(2) == 0)
    def _():
        state_ref[...] = jnp.zeros_like(state_ref)
        tail_ref[...] = jnp.zeros_like(tail_ref)

    q_raw, k_raw, v_raw = q_ref[...], k_ref[...], v_ref[...]
    q = _conv_silu(q_raw, tail_ref[0], cq_ref[...])
    k = _conv_silu(k_raw, tail_ref[1], ck_ref[...])
    v = _conv_silu(v_raw, tail_ref[2], cv_ref[...])
    tail_ref[0] = q_raw[GDN_TB - SUBLANES:, :]
    tail_ref[1] = k_raw[GDN_TB - SUBLANES:, :]
    tail_ref[2] = v_raw[GDN_TB - SUBLANES:, :]

    q = q * lax.rsqrt(jnp.sum(q * q, axis=-1, keepdims=True) + EPS) * (B_HEAD_DIM ** -0.5)
    k = k * lax.rsqrt(jnp.sum(k * k, axis=-1, keepdims=True) + EPS)

    gate = gate_ref[...]
    lane = lax.broadcasted_iota(jnp.int32, gate.shape, 1)
    beta_all = jax.nn.sigmoid(gate)
    g_all = -jnp.exp(alog_ref[...]) * jax.nn.softplus(gate + dt_ref[...])
    beta = jnp.sum(jnp.where(lane == h, beta_all, 0.0), axis=-1, keepdims=True)
    g = jnp.sum(jnp.where(lane == h + B_HEADS, g_all, 0.0), axis=-1, keepdims=True)

    row = lax.broadcasted_iota(jnp.int32, (c, c), 0)
    col = lax.broadcasted_iota(jnp.int32, (c, c), 1)
    incl = row >= col
    strict = row > col
    tri = jnp.where(incl, 1.0, 0.0).astype(BF16)
    eye = jnp.where(row == col, 1.0, 0.0).astype(F32)

    onorm = onorm_ref[...]
    z = z_ref[...]
    state = state_ref[...]
    pre = []
    for n in range(GDN_NC):
        sl = slice(n * c, (n + 1) * c)
        qc, kc, vc, bc, gc = q[sl], k[sl], v[sl], beta[sl], g[sl]
        gam = _dot_exact_lhs(tri, jnp.broadcast_to(gc, (c, LANES)))
        dmat = _dot_exact_lhs(tri, jnp.where(strict, jnp.broadcast_to(gc, (c, c)), 0.0))
        decay = jnp.exp(jnp.where(incl, dmat, -jnp.inf))
        kb = kc.astype(BF16)
        kk = _dot_nt(kb, kb)
        qk = _dot_nt(qc.astype(BF16), kb) * decay
        lmat = jnp.where(strict, bc * kk * decay, 0.0)
        x = -lmat
        tinv = eye + x
        for _ in range(5):
            xb = x.astype(BF16)
            x = _dot(xb, xb)
            tinv = tinv + _dot(tinv.astype(BF16), x.astype(BF16))
        egam = jnp.exp(gam)
        rhs = jnp.concatenate([vc * bc, kc * (bc * egam)], axis=-1).astype(BF16)
        sol = _dot(tinv.astype(BF16), rhs)
        u, w = sol[:, :B_HEAD_DIM], sol[:, B_HEAD_DIM:]
        gam_last = gam[c - 1:c, :]
        q_dec = qc * egam
        k_dec = kc * jnp.exp(gam_last - gam)
        lhs = jnp.concatenate([w, q_dec], axis=0).astype(BF16)
        pre.append((u, lhs, qk.astype(BF16), k_dec.astype(BF16), jnp.exp(gam_last)))

    for n in range(GDN_NC):
        u, lhs, qkb, kdb, g_last = pre[n]
        r = _dot(lhs, state.astype(BF16))
        v_new = u - r[:c]
        vnb = v_new.astype(BF16)
        o = r[c:] + _dot(qkb, vnb)
        state = state * g_last + _dot_tn(kdb, vnb)
        o = o * lax.rsqrt(jnp.mean(o * o, axis=-1, keepdims=True) + EPS) * onorm
        o = o * _silu(z[n * c:(n + 1) * c])
        o_ref[n * c:(n + 1) * c, :] = o.astype(o_ref.dtype)
    state_ref[...] = state


def _gdn(proj, gates, conv_w, alog_row, dt_row, onorm, batch, seq):
    proj3 = proj.reshape(batch, seq, proj.shape[-1])
    gates3 = gates.reshape(batch, seq, LANES)
    tb = GDN_TB
    tok = lambda blk: pl.BlockSpec((None, tb, LANES), lambda b, h, t: (b, t, blk + h))
    cw = lambda blk: pl.BlockSpec((CONV_K, LANES), lambda b, h, t: (0, blk + h))
    row = pl.BlockSpec((1, LANES), lambda b, h, t: (0, 0))
    out = pl.pallas_call(
        _gdn_kernel,
        out_shape=jax.ShapeDtypeStruct((batch, seq, B_W), BF16),
        grid=(batch, B_HEADS, seq // tb),
        in_specs=[tok(BLK_QB), tok(BLK_KB), tok(BLK_VB), tok(BLK_Z),
                  pl.BlockSpec((None, tb, LANES), lambda b, h, t: (b, t, 0)),
                  cw(0), cw(B_HEADS), cw(2 * B_HEADS), row, row, row],
        out_specs=pl.BlockSpec((None, tb, LANES), lambda b, h, t: (b, t, h)),
        scratch_shapes=[pltpu.VMEM((B_HEAD_DIM, B_HEAD_DIM), F32),
                        pltpu.VMEM((3, SUBLANES, LANES), F32)],
        compiler_params=_params(("parallel", "parallel", "arbitrary")),
        name="gated_deltanet",
    )(proj3, proj3, proj3, proj3, gates3, conv_w, conv_w, conv_w, alog_row, dt_row, onorm)
    return out.reshape(batch * seq, B_W)


SGU_TB = 512


def _sgu_kernel(u_ref, v_ref, lg_ref, lb_ref, ws_ref, bs_ref, o_ref):
    cc = C_CHUNK
    v = v_ref[...]
    mu = jnp.mean(v, axis=-1, keepdims=True)
    vc = v - mu
    var = jnp.mean(vc * vc, axis=-1, keepdims=True)
    y = (vc * lax.rsqrt(var + EPS) * lg_ref[...] + lb_ref[...]).astype(BF16)
    row = lax.broadcasted_iota(jnp.int32, (cc, cc), 0)
    col = lax.broadcasted_iota(jnp.int32, (cc, cc), 1)
    bs = bs_ref[...]
    for gi in range(C_GROUPS):
        ws = jnp.where(row >= col, ws_ref[gi], 0.0).astype(BF16)
        bias = bs[:, gi:gi + 1]
        cs = slice(gi * C_GROUP_DIM, (gi + 1) * C_GROUP_DIM)
        for n in range(SGU_TB // cc):
            rs = slice(n * cc, (n + 1) * cc)
            mixed = _dot(ws, y[rs, cs]) + bias
            o_ref[rs, cs] = (u_ref[rs, cs] * mixed).astype(o_ref.dtype)


def _sgu(zz, ln_g, ln_b, w_s, b_s_t):
    n = zz.shape[0]
    d = D_MODEL
    tb = SGU_TB
    return pl.pallas_call(
        _sgu_kernel,
        out_shape=jax.ShapeDtypeStruct((n, d), BF16),
        grid=(n // tb,),
        in_specs=[pl.BlockSpec((tb, d), lambda i: (i, 0)),
                  pl.BlockSpec((tb, d), lambda i: (i, 1)),
                  pl.BlockSpec((1, d), lambda i: (0, 0)),
                  pl.BlockSpec((1, d), lambda i: (0, 0)),
                  pl.BlockSpec((C_GROUPS, C_CHUNK, C_CHUNK), lambda i: (0, 0, 0)),
                  pl.BlockSpec((C_CHUNK, C_GROUPS), lambda i: (0, 0))],
        out_specs=pl.BlockSpec((tb, d), lambda i: (i, 0)),
        compiler_params=_params(("parallel",)),
        name="spatial_gating",
    )(zz, zz, ln_g, ln_b, w_s, b_s_t)


def kernel(x, even_norm, even_w_in, even_conv, even_a_log, even_dt_bias, even_sinks,
           even_onorm, even_w_out, odd_norm, odd_w_in, odd_ln_g, odd_ln_b, odd_w_s,
           odd_b_s, odd_w_out, ffn_norm, ffn_w_gate, ffn_w_up, ffn_w_down, final_norm):
    batch, seq, d = x.shape
    n = batch * seq
    h = x.reshape(n, d)
    row = lambda a: a.reshape(1, -1).astype(F32)

    w_in = even_w_in[0]
    w_main = w_in[:, :EVEN_MAIN].astype(BF16)
    w_gate = jnp.pad(w_in[:, EVEN_MAIN:], ((0, 0), (0, LANES - 2 * B_HEADS))).astype(BF16)
    proj, gates = _norm_proj_even(h, row(even_norm[0]), w_main, w_gate)
    out_a = _swa(proj, even_sinks[0].astype(F32), batch, seq)
    pad_b = lambda a: jnp.pad(a.astype(F32), (B_HEADS, LANES - 2 * B_HEADS)).reshape(1, LANES)
    out_b = _gdn(proj, gates, even_conv[0], pad_b(even_a_log[0]), pad_b(even_dt_bias[0]),
                 row(even_onorm[0]), batch, seq)
    w_out = even_w_out[0].astype(BF16)
    h = _proj_residual([out_a, out_b], [w_out[:A_Q], w_out[A_Q:]], h)
    h = _ffn(h, row(ffn_norm[0]), ffn_w_gate[0].astype(BF16), ffn_w_up[0].astype(BF16),
             ffn_w_down[0].astype(BF16), row(final_norm), final_norm=False)

    zz = _norm_proj_gelu(h, row(odd_norm[0]), odd_w_in[0].astype(BF16))
    gated = _sgu(zz, row(odd_ln_g[0]), row(odd_ln_b[0]), odd_w_s[0], odd_b_s[0].T)
    h = _proj_residual([gated], [odd_w_out[0].astype(BF16)], h)
    h = _ffn(h, row(ffn_norm[1]), ffn_w_gate[1].astype(BF16), ffn_w_up[1].astype(BF16),
             ffn_w_down[1].astype(BF16), row(final_norm), final_norm=True)
    return h.reshape(batch, seq, d)
```

```python
import functools

import jax
import jax.numpy as jnp
from jax import lax
from jax.experimental import pallas as pl
from jax.experimental.pallas import tpu as pltpu

F32 = jnp.float32
BF16 = jnp.bfloat16

D_MODEL = 2048
A_HEADS = 16
A_KV_HEADS = 2
A_HEAD_DIM = 64
WINDOW = 128
B_HEADS = 8
B_HEAD_DIM = 128
CONV_K = 4
DN_CHUNK = 64
DN_CHUNK_LOG2 = 6
C_GROUPS = 8
C_CHUNK = 128
C_GROUP_DIM = D_MODEL // C_GROUPS
EPS = 1e-6

LANES = 128
SUBLANES = 8
A_Q = A_HEADS * A_HEAD_DIM
A_KV = A_KV_HEADS * A_HEAD_DIM
B_W = B_HEADS * B_HEAD_DIM
EVEN_MAIN = A_Q + 2 * A_KV + 4 * B_W
BLK_Z = A_Q // LANES
BLK_QB = BLK_Z + B_HEADS
BLK_KB = BLK_QB + B_HEADS
BLK_VB = BLK_KB + B_HEADS
BLK_KA = BLK_VB + B_HEADS
BLK_VA = BLK_KA + A_KV // LANES
GATE_BETA = 0
GATE_GAM = B_HEADS
GATE_LAST = 2 * B_HEADS
GATE_ROWS = 3 * B_HEADS

VMEM_LIMIT = 56 * 1024 * 1024


def _params(sem, vmem=VMEM_LIMIT):
    return pltpu.CompilerParams(dimension_semantics=sem, vmem_limit_bytes=vmem)


def _dot(a, b):
    return jnp.dot(a, b, preferred_element_type=F32)


def _dot_nt(a, b):
    return lax.dot_general(a, b, (((1,), (1,)), ((), ())), preferred_element_type=F32)


def _rms_rows(x, g):
    ms = jnp.mean(x * x, axis=-1, keepdims=True)
    return x * lax.rsqrt(ms + EPS) * g


def _silu(x):
    return x * jax.nn.sigmoid(x)


def _gelu(x):
    return x * (lax.erf(x * (2.0 ** -0.5)) + 1.0) * 0.5


def _norm_proj_kernel(x_ref, g_ref, w_ref, wg_ref, o_ref, og_ref, hn_ref):
    @pl.when(pl.program_id(1) == 0)
    def _():
        hn = _rms_rows(x_ref[...], g_ref[...]).astype(BF16)
        hn_ref[...] = hn
        og_ref[...] = _dot(hn, wg_ref[...])

    o_ref[...] = _dot(hn_ref[...], w_ref[...])


def _norm_proj_even(x, g, w_main, w_gate, *, tm=1024, tn=768):
    n, d = x.shape
    nout = w_main.shape[1]
    return pl.pallas_call(
        _norm_proj_kernel,
        out_shape=(jax.ShapeDtypeStruct((n, nout), F32),
                   jax.ShapeDtypeStruct((n, LANES), F32)),
        grid=(n // tm, nout // tn),
        in_specs=[pl.BlockSpec((tm, d), lambda i, j: (i, 0)),
                  pl.BlockSpec((1, d), lambda i, j: (0, 0)),
                  pl.BlockSpec((d, tn), lambda i, j: (0, j)),
                  pl.BlockSpec((d, LANES), lambda i, j: (0, 0))],
        out_specs=(pl.BlockSpec((tm, tn), lambda i, j: (i, j)),
                   pl.BlockSpec((tm, LANES), lambda i, j: (i, 0))),
        scratch_shapes=[pltpu.VMEM((tm, d), BF16)],
        compiler_params=_params(("parallel", "arbitrary")),
        name="norm_proj_even",
    )(x, g, w_main, w_gate)


def _norm_gelu_kernel(x_ref, g_ref, w_ref, o_ref, hn_ref):
    @pl.when(pl.program_id(1) == 0)
    def _():
        hn_ref[...] = _rms_rows(x_ref[...], g_ref[...]).astype(BF16)

    o_ref[...] = _gelu(_dot(hn_ref[...], w_ref[...]))


def _norm_proj_gelu(x, g, w, *, tm=1024, tn=1024):
    n, d = x.shape
    nout = w.shape[1]
    return pl.pallas_call(
        _norm_gelu_kernel,
        out_shape=jax.ShapeDtypeStruct((n, nout), F32),
        grid=(n // tm, nout // tn),
        in_specs=[pl.BlockSpec((tm, d), lambda i, j: (i, 0)),
                  pl.BlockSpec((1, d), lambda i, j: (0, 0)),
                  pl.BlockSpec((d, tn), lambda i, j: (0, j))],
        out_specs=pl.BlockSpec((tm, tn), lambda i, j: (i, j)),
        scratch_shapes=[pltpu.VMEM((tm, d), BF16)],
        compiler_params=_params(("parallel", "arbitrary")),
        name="norm_proj_gelu",
    )(x, g, w)


def _proj_res_kernel(*refs, n_in):
    a_refs = refs[:n_in]
    w_refs = refs[n_in:2 * n_in]
    res_ref = refs[2 * n_in]
    o_ref = refs[2 * n_in + 1]
    acc = res_ref[...]
    for a_ref, w_ref in zip(a_refs, w_refs):
        acc = acc + _dot(a_ref[...], w_ref[...])
    o_ref[...] = acc


def _proj_residual(a_list, w_list, res, *, tm=512):
    n, d = res.shape
    n_in = len(a_list)
    in_specs = ([pl.BlockSpec((tm, a.shape[1]), lambda i: (i, 0)) for a in a_list]
                + [pl.BlockSpec(w.shape, lambda i: (0, 0)) for w in w_list]
                + [pl.BlockSpec((tm, d), lambda i: (i, 0))])
    return pl.pallas_call(
        functools.partial(_proj_res_kernel, n_in=n_in),
        out_shape=jax.ShapeDtypeStruct((n, d), F32),
        grid=(n // tm,),
        in_specs=in_specs,
        out_specs=pl.BlockSpec((tm, d), lambda i: (i, 0)),
        compiler_params=_params(("parallel",)),
        name="proj_residual",
    )(*a_list, *w_list, res)


def _ffn_kernel(x_ref, g_ref, wg_ref, wu_ref, wd_ref, fg_ref, o_ref, hn_ref, *, final_norm):
    f = pl.program_id(1)

    @pl.when(f == 0)
    def _():
        x = x_ref[...]
        hn_ref[...] = _rms_rows(x, g_ref[...]).astype(BF16)
        o_ref[...] = x

    hn = hn_ref[...]
    a = (_silu(_dot(hn, wg_ref[...])) * _dot(hn, wu_ref[...])).astype(BF16)
    o_ref[...] += _dot(a, wd_ref[...])

    if final_norm:
        @pl.when(f == pl.num_programs(1) - 1)
        def _():
            o_ref[...] = _rms_rows(o_ref[...], fg_ref[...])


def _ffn(x, g, w_gate, w_up, w_down, final_g, *, final_norm, tm=1024, tf=512):
    n, d = x.shape
    dff = w_gate.shape[1]
    return pl.pallas_call(
        functools.partial(_ffn_kernel, final_norm=final_norm),
        out_shape=jax.ShapeDtypeStruct((n, d), F32),
        grid=(n // tm, dff // tf),
        in_specs=[pl.BlockSpec((tm, d), lambda i, f: (i, 0)),
                  pl.BlockSpec((1, d), lambda i, f: (0, 0)),
                  pl.BlockSpec((d, tf), lambda i, f: (0, f)),
                  pl.BlockSpec((d, tf), lambda i, f: (0, f)),
                  pl.BlockSpec((tf, d), lambda i, f: (f, 0)),
                  pl.BlockSpec((1, d), lambda i, f: (0, 0))],
        out_specs=pl.BlockSpec((tm, d), lambda i, f: (i, 0)),
        scratch_shapes=[pltpu.VMEM((tm, d), BF16)],
        compiler_params=_params(("parallel", "arbitrary")),
        name="swiglu_ffn",
    )(x, g, w_gate, w_up, w_down, final_g)


def _swa_kernel(sink_ref, q_ref, k_ref, v_ref, kp_ref, vp_ref, o_ref):
    first = pl.program_id(1) == 0
    w = WINDOW
    grp = A_HEADS // A_KV_HEADS
    lane = lax.broadcasted_iota(jnp.int32, (w, LANES), 1)
    low = lane < A_HEAD_DIM
    row = lax.broadcasted_iota(jnp.int32, (w, w), 0)
    col = lax.broadcasted_iota(jnp.int32, (w, w), 1)
    cur_ok = jnp.concatenate([col <= row] * 2, axis=0)

    def dup_half(x, head):
        swapped = pltpu.roll(x, A_HEAD_DIM, 1)
        if head == 0:
            return jnp.where(low, x, swapped)
        return jnp.where(low, swapped, x)

    kc, vc, kp, vp = k_ref[...], v_ref[...], kp_ref[...], vp_ref[...]
    k2s = [jnp.concatenate([dup_half(kc, kvh), dup_half(kp, kvh)], axis=0).astype(BF16)
           for kvh in range(A_KV_HEADS)]
    v2s = [jnp.concatenate([dup_half(vc, kvh), dup_half(vp, kvh)], axis=0).astype(BF16)
           for kvh in range(A_KV_HEADS)]
    scale = A_HEAD_DIM ** -0.5
    zero = jnp.zeros((), BF16)
    pairs = range(A_HEADS // 2)
    s2s, sinks = [], []
    for pr in pairs:
        pair = q_ref[:, pr * LANES:(pr + 1) * LANES] * scale
        qst = jnp.concatenate([jnp.where(low, pair, 0.0), jnp.where(low, 0.0, pair)], axis=0)
        s2s.append(_dot_nt(qst.astype(BF16), k2s[2 * pr // grp]))
        sinks.append(jnp.concatenate([jnp.full((w, w), sink_ref[2 * pr], F32),
                                      jnp.full((w, w), sink_ref[2 * pr + 1], F32)], axis=0))
    p2s = []
    for s2, sink in zip(s2s, sinks):
        s = jnp.where(cur_ok, s2[:, :w], jnp.where(first, -jnp.inf, s2[:, w:]))
        m = jnp.maximum(jnp.max(s, axis=-1, keepdims=True), sink)
        p = jnp.exp(s - m)
        denom = jnp.sum(p, axis=-1, keepdims=True) + jnp.exp(sink - m)
        pb = (p * (1.0 / denom)).astype(BF16)
        p2s.append(jnp.concatenate([jnp.where(cur_ok, pb, zero), jnp.where(cur_ok, zero, pb)], axis=-1))
    for pr, p2 in zip(pairs, p2s):
        o = _dot(p2, v2s[2 * pr // grp])
        o_ref[:, pr * LANES:(pr + 1) * LANES] = jnp.where(low, o[:w], o[w:]).astype(o_ref.dtype)


def _swa(proj, sinks, batch, seq):
    nb = seq // WINDOW
    proj3 = proj.reshape(batch, seq, proj.shape[-1])
    spec = lambda blk, prev: pl.BlockSpec(
        (None, WINDOW, LANES),
        (lambda b, n: (b, jnp.maximum(n - 1, 0), blk)) if prev else (lambda b, n: (b, n, blk)))
    out = pl.pallas_call(
        _swa_kernel,
        out_shape=jax.ShapeDtypeStruct((batch, seq, A_Q), BF16),
        grid=(batch, nb),
        in_specs=[pl.BlockSpec(memory_space=pltpu.SMEM),
                  pl.BlockSpec((None, WINDOW, A_Q), lambda b, n: (b, n, 0)),
                  spec(BLK_KA, False), spec(BLK_VA, False),
                  spec(BLK_KA, True), spec(BLK_VA, True)],
        out_specs=pl.BlockSpec((None, WINDOW, A_Q), lambda b, n: (b, n, 0)),
        compiler_params=_params(("parallel", "arbitrary")),
        name="swa_sink_attention",
    )(sinks, proj3, proj3, proj3, proj3, proj3)
    return out.reshape(batch * seq, A_Q)


GATE_TB = 512
GDN_TB = 1024
GDN_NC = GDN_TB // DN_CHUNK
GDN_GRP = 128
GRP_NC = GDN_GRP // DN_CHUNK
SCAN_TB = 512
SCAN_NC = SCAN_TB // DN_CHUNK


def _split3(x):
    hi = x.astype(BF16)
    r1 = x - hi.astype(F32)
    mid = r1.astype(BF16)
    lo = (r1 - mid.astype(F32)).astype(BF16)
    return hi, mid, lo


def _gate_prep_kernel(gate_ref, alog_ref, dt_ref, o_ref, ot_ref):
    tb = GATE_TB
    gate = gate_ref[...]
    lane = lax.broadcasted_iota(jnp.int32, (tb, LANES), 1)
    beta = jax.nn.sigmoid(gate)
    g = -jnp.exp(alog_ref[...]) * jax.nn.softplus(gate + dt_ref[...])
    row = lax.broadcasted_iota(jnp.int32, (tb, tb), 0)
    col = lax.broadcasted_iota(jnp.int32, (tb, tb), 1)
    same = (row >> DN_CHUNK_LOG2) == (col >> DN_CHUNK_LOG2)
    tri = jnp.where(jnp.logical_and(same, row >= col), 1.0, 0.0).astype(BF16)
    ones = jnp.where(same, 1.0, 0.0).astype(BF16)
    r = _dot(jnp.concatenate([tri, ones], axis=0), jnp.concatenate(_split3(g), axis=-1))
    s = r[:, :LANES] + r[:, LANES:2 * LANES] + r[:, 2 * LANES:]
    out = jnp.where(lane < GATE_GAM, beta, jnp.where(lane < GATE_LAST, s[:tb], s[tb:]))
    o_ref[...] = out
    ot_ref[...] = out.T[:GATE_ROWS, :]


def _gate_prep(gates, alog_row, dt_row):
    n = gates.shape[0]
    tb = GATE_TB
    row = pl.BlockSpec((1, LANES), lambda i: (0, 0))
    return pl.pallas_call(
        _gate_prep_kernel,
        out_shape=(jax.ShapeDtypeStruct((n, LANES), F32),
                   jax.ShapeDtypeStruct((GATE_ROWS, n), F32)),
        grid=(n // tb,),
        in_specs=[pl.BlockSpec((tb, LANES), lambda i: (i, 0)), row, row],
        out_specs=(pl.BlockSpec((tb, LANES), lambda i: (i, 0)),
                   pl.BlockSpec((GATE_ROWS, tb), lambda i: (0, i))),
        compiler_params=_params(("parallel",)),
        name="gdn_gate_prep",
    )(gates, alog_row, dt_row)


def _conv_silu(x, tail, w):
    acc = x * w[CONV_K - 1:CONV_K, :]
    x8 = x[0:SUBLANES, :]
    acc8 = x8 * w[CONV_K - 1:CONV_K, :]
    row8 = lax.broadcasted_iota(jnp.int32, (SUBLANES, LANES), 0)
    for j in range(CONV_K - 1):
        s = CONV_K - 1 - j
        wj = w[j:j + 1, :]
        acc = acc + pltpu.roll(x, s, 0) * wj
        head = jnp.where(row8 < s, pltpu.roll(tail, s, 0), pltpu.roll(x8, s, 0))
        acc8 = acc8 + head * wj
    return _silu(jnp.concatenate([acc8, acc[SUBLANES:, :]], axis=0))


def _lane_col(x, idx):
    lane = lax.broadcasted_iota(jnp.int32, x.shape, 1)
    return jnp.sum(jnp.where(lane == idx, x, 0.0), axis=-1, keepdims=True)


def _gdn_pre_kernel(q_ref, k_ref, v_ref, hq_ref, hk_ref, hv_ref, gc_ref, gr_ref,
                    cq_ref, ck_ref, cv_ref, kw_ref, nn_ref, qe_ref, ol_ref, gl_ref):
    h = pl.program_id(1)
    first = pl.program_id(2) == 0
    gs = GDN_GRP

    def conv(x_ref, halo_ref, w_ref):
        tail = jnp.where(first, 0.0, halo_ref[...])
        return _conv_silu(x_ref[...], tail, w_ref[...])

    q = conv(q_ref, hq_ref, cq_ref)
    k = conv(k_ref, hk_ref, ck_ref)
    v = conv(v_ref, hv_ref, cv_ref)
    q = q * lax.rsqrt(jnp.sum(q * q, axis=-1, keepdims=True) + EPS) * (B_HEAD_DIM ** -0.5)
    k = k * lax.rsqrt(jnp.sum(k * k, axis=-1, keepdims=True) + EPS)

    gall = gc_ref[...]
    beta = _lane_col(gall, h + GATE_BETA)
    gam = _lane_col(gall, h + GATE_GAM)
    gam_end = _lane_col(gall, h + GATE_LAST)
    gam_row = gr_ref[pl.ds(h + GATE_GAM, 1), :]
    egam = jnp.exp(gam)
    q_dec = q * egam
    k_dec = k * jnp.exp(gam_end - gam)
    rhs_all = jnp.concatenate([v * beta, k * (beta * egam)], axis=-1)
    ends = gc_ref[pl.ds(DN_CHUNK - 1, GDN_NC, stride=DN_CHUNK), :]
    gl_ref[...] = jnp.broadcast_to(jnp.exp(_lane_col(ends, h + GATE_GAM)), (GDN_NC, LANES))

    row = lax.broadcasted_iota(jnp.int32, (gs, gs), 0)
    col = lax.broadcasted_iota(jnp.int32, (gs, gs), 1)
    same = (row >> DN_CHUNK_LOG2) == (col >> DN_CHUNK_LOG2)
    incl = jnp.logical_and(same, row >= col)
    strict = jnp.logical_and(same, row > col)
    trow = lax.broadcasted_iota(jnp.int32, (GRP_NC * B_HEAD_DIM, gs), 0)
    tcol = lax.broadcasted_iota(jnp.int32, (GRP_NC * B_HEAD_DIM, gs), 1)
    own = (trow >> 7) == (tcol >> DN_CHUNK_LOG2)

    groups = range(GDN_TB // gs)
    sls = [slice(grp * gs, (grp + 1) * gs) for grp in groups]
    kbs = [k[sl].astype(BF16) for sl in sls]
    scs = [_dot_nt(jnp.concatenate([q[sl].astype(BF16), kb], axis=0), kb)
           for sl, kb in zip(sls, kbs)]
    decays = [jnp.exp(jnp.where(incl, gam[sl] - gam_row[:, sl], -jnp.inf)) for sl in sls]
    qks = [(sc[:gs] * dec).astype(BF16) for sc, dec in zip(scs, decays)]
    xs = [jnp.where(strict, (-beta[sl]) * sc[gs:] * dec, 0.0)
          for sl, sc, dec in zip(sls, scs, decays)]
    ys = [rhs_all[sl] for sl in sls]
    for _ in range(DN_CHUNK_LOG2 - 1):
        xbs = [x.astype(BF16) for x in xs]
        rs = [_dot(xb, jnp.concatenate([xb, y.astype(BF16)], axis=-1))
              for xb, y in zip(xbs, ys)]
        xs = [r[:, :gs] for r in rs]
        ys = [y + r[:, gs:] for y, r in zip(ys, rs)]
    ys = [y + _dot(x.astype(BF16), y.astype(BF16)) for x, y in zip(xs, ys)]
    ybs = [y.astype(BF16) for y in ys]
    es = [_dot(qk, yb) for qk, yb in zip(qks, ybs)]
    kdt4s = [jnp.where(own, jnp.concatenate([k_dec[sl].T] * GRP_NC, axis=0), 0.0).astype(BF16)
             for sl in sls]
    fs = [_dot(kdt4, yb) for kdt4, yb in zip(kdt4s, ybs)]
    for grp, sl, e, f in zip(groups, sls, es, fs):
        ol_ref[sl, :] = e[:, :B_HEAD_DIM]
        qe_ref[sl, :] = (q_dec[sl] - e[:, B_HEAD_DIM:]).astype(qe_ref.dtype)
        rws = slice(grp * GRP_NC * B_HEAD_DIM, (grp + 1) * GRP_NC * B_HEAD_DIM)
        nn_ref[rws, :] = f[:, :B_HEAD_DIM]
        kw_ref[rws, :] = f[:, B_HEAD_DIM:].astype(kw_ref.dtype)


def _gdn_scan_kernel(kw_ref, nn_ref, qe_ref, ol_ref, gl_ref, z_ref, onorm_ref, o_ref, state_ref):
    c = DN_CHUNK
    dk = B_HEAD_DIM

    @pl.when(pl.program_id(1) == 0)
    def _():
        state_ref[...] = jnp.zeros_like(state_ref)

    onorm = onorm_ref[...]
    for n in range(SCAN_NC):
        ts = slice(n * c, (n + 1) * c)
        ds = slice(n * dk, (n + 1) * dk)
        for h in range(B_HEADS):
            hs = slice(h * dk, (h + 1) * dk)
            s = state_ref[h]
            r = _dot(jnp.concatenate([kw_ref[h, ds, :], qe_ref[h, ts, :]], axis=0), s.astype(BF16))
            state_ref[h] = s * gl_ref[h, n:n + 1, :] + nn_ref[h, ds, :] - r[:dk]
            o = ol_ref[h, ts, :] + r[dk:]
            o = o * lax.rsqrt(jnp.mean(o * o, axis=-1, keepdims=True) + EPS) * onorm
            o_ref[ts, hs] = (o * _silu(z_ref[ts, hs])).astype(o_ref.dtype)


def _gdn(proj, gates, conv_w, alog_row, dt_row, onorm, batch, seq):
    gcol, grow = _gate_prep(gates, alog_row, dt_row)
    proj3 = proj.reshape(batch, seq, proj.shape[-1])
    gcol3 = gcol.reshape(batch, seq, LANES)
    tb = GDN_TB
    nt = seq // tb
    nch = seq // DN_CHUNK
    dk = B_HEAD_DIM
    tok = lambda blk: pl.BlockSpec((None, tb, LANES), lambda b, h, t: (b, t, blk + h))
    halo = lambda blk: pl.BlockSpec(
        (None, SUBLANES, LANES),
        lambda b, h, t: (b, jnp.maximum(t * (tb // SUBLANES) - 1, 0), blk + h))
    cw = lambda blk: pl.BlockSpec((CONV_K, LANES), lambda b, h, t: (0, blk + h))
    per_chunk = lambda rows: pl.BlockSpec((None, None, rows, LANES), lambda b, h, t: (b, h, t, 0))
    kw, nn, qe, ol, gl = pl.pallas_call(
        _gdn_pre_kernel,
        out_shape=(jax.ShapeDtypeStruct((batch, B_HEADS, nch * dk, dk), BF16),
                   jax.ShapeDtypeStruct((batch, B_HEADS, nch * dk, dk), F32),
                   jax.ShapeDtypeStruct((batch, B_HEADS, seq, dk), BF16),
                   jax.ShapeDtypeStruct((batch, B_HEADS, seq, dk), F32),
                   jax.ShapeDtypeStruct((batch, B_HEADS, nch, LANES), F32)),
        grid=(batch, B_HEADS, nt),
        in_specs=[tok(BLK_QB), tok(BLK_KB), tok(BLK_VB),
                  halo(BLK_QB), halo(BLK_KB), halo(BLK_VB),
                  pl.BlockSpec((None, tb, LANES), lambda b, h, t: (b, t, 0)),
                  pl.BlockSpec((GATE_ROWS, tb), lambda b, h, t: (0, b * nt + t)),
                  cw(0), cw(B_HEADS), cw(2 * B_HEADS)],
        out_specs=(per_chunk(GDN_NC * dk), per_chunk(GDN_NC * dk), per_chunk(tb), per_chunk(tb),
                   per_chunk(GDN_NC)),
        compiler_params=_params(("parallel", "parallel", "parallel")),
        name="gdn_pre",
    )(proj3, proj3, proj3, proj3, proj3, proj3, gcol3, grow, conv_w, conv_w, conv_w)

    heads = lambda rows: pl.BlockSpec((None, B_HEADS, rows, LANES), lambda b, t: (b, 0, t, 0))
    ts = SCAN_TB
    out = pl.pallas_call(
        _gdn_scan_kernel,
        out_shape=jax.ShapeDtypeStruct((batch, seq, B_W), BF16),
        grid=(batch, seq // ts),
        in_specs=[heads(SCAN_NC * dk), heads(SCAN_NC * dk), heads(ts), heads(ts), heads(SCAN_NC),
                  pl.BlockSpec((None, ts, B_W), lambda b, t: (b, t, BLK_Z * LANES // B_W)),
                  pl.BlockSpec((1, dk), lambda b, t: (0, 0))],
        out_specs=pl.BlockSpec((None, ts, B_W), lambda b, t: (b, t, 0)),
        scratch_shapes=[pltpu.VMEM((B_HEADS, dk, dk), F32)],
        compiler_params=_params(("parallel", "arbitrary")),
        name="gdn_scan",
    )(kw, nn, qe, ol, gl, proj3, onorm)
    return out.reshape(batch * seq, B_W)


SGU_TB = 512


def _sgu_kernel(u_ref, v_ref, lg_ref, lb_ref, ws_ref, bs_ref, o_ref):
    cc = C_CHUNK
    v = v_ref[...]
    mu = jnp.mean(v, axis=-1, keepdims=True)
    vc = v - mu
    var = jnp.mean(vc * vc, axis=-1, keepdims=True)
    y = (vc * lax.rsqrt(var + EPS) * lg_ref[...] + lb_ref[...]).astype(BF16)
    row = lax.broadcasted_iota(jnp.int32, (cc, cc), 0)
    col = lax.broadcasted_iota(jnp.int32, (cc, cc), 1)
    bs = bs_ref[...]
    for gi in range(C_GROUPS):
        ws = jnp.where(row >= col, ws_ref[gi], 0.0).astype(BF16)
        bias = bs[:, gi:gi + 1]
        cs = slice(gi * C_GROUP_DIM, (gi + 1) * C_GROUP_DIM)
        for n in range(SGU_TB // cc):
            rs = slice(n * cc, (n + 1) * cc)
            mixed = _dot(ws, y[rs, cs]) + bias
            o_ref[rs, cs] = (u_ref[rs, cs] * mixed).astype(o_ref.dtype)


def _sgu(zz, ln_g, ln_b, w_s, b_s_t):
    n = zz.shape[0]
    d = D_MODEL
    tb = SGU_TB
    return pl.pallas_call(
        _sgu_kernel,
        out_shape=jax.ShapeDtypeStruct((n, d), BF16),
        grid=(n // tb,),
        in_specs=[pl.BlockSpec((tb, d), lambda i: (i, 0)),
                  pl.BlockSpec((tb, d), lambda i: (i, 1)),
                  pl.BlockSpec((1, d), lambda i: (0, 0)),
                  pl.BlockSpec((1, d), lambda i: (0, 0)),
                  pl.BlockSpec((C_GROUPS, C_CHUNK, C_CHUNK), lambda i: (0, 0, 0)),
                  pl.BlockSpec((C_CHUNK, C_GROUPS), lambda i: (0, 0))],
        out_specs=pl.BlockSpec((tb, d), lambda i: (i, 0)),
        compiler_params=_params(("parallel",)),
        name="spatial_gating",
    )(zz, zz, ln_g, ln_b, w_s, b_s_t)


def kernel(x, even_norm, even_w_in, even_conv, even_a_log, even_dt_bias, even_sinks,
           even_onorm, even_w_out, odd_norm, odd_w_in, odd_ln_g, odd_ln_b, odd_w_s,
           odd_b_s, odd_w_out, ffn_norm, ffn_w_gate, ffn_w_up, ffn_w_down, final_norm):
    batch, seq, d = x.shape
    n = batch * seq
    h = x.reshape(n, d)
    row = lambda a: a.reshape(1, -1).astype(F32)

    w_in = even_w_in[0]
    c_ka, c_qb, c_z, c_beta = A_Q, A_Q + 2 * A_KV, A_Q + 2 * A_KV + 3 * B_W, EVEN_MAIN
    w_main = jnp.concatenate([w_in[:, :c_ka], w_in[:, c_z:c_beta], w_in[:, c_qb:c_z],
                              w_in[:, c_ka:c_qb]], axis=1).astype(BF16)
    w_a = w_in[:, c_beta + B_HEADS:]
    w_gate = jnp.pad(jnp.concatenate([w_in[:, c_beta:], w_a], axis=1),
                     ((0, 0), (0, LANES - GATE_ROWS))).astype(BF16)
    proj, gates = _norm_proj_even(h, row(even_norm[0]), w_main, w_gate)
    out_a = _swa(proj, even_sinks[0].astype(F32), batch, seq)

    def gate_row(a):
        a = a.astype(F32)
        return jnp.pad(jnp.concatenate([a, a]), (GATE_GAM, LANES - GATE_ROWS)).reshape(1, LANES)

    out_b = _gdn(proj, gates, even_conv[0], gate_row(even_a_log[0]), gate_row(even_dt_bias[0]),
                 row(even_onorm[0]), batch, seq)
    w_out = even_w_out[0].astype(BF16)
    h = _proj_residual([out_a, out_b], [w_out[:A_Q], w_out[A_Q:]], h)
    h = _ffn(h, row(ffn_norm[0]), ffn_w_gate[0].astype(BF16), ffn_w_up[0].astype(BF16),
             ffn_w_down[0].astype(BF16), row(final_norm), final_norm=False)

    zz = _norm_proj_gelu(h, row(odd_norm[0]), odd_w_in[0].astype(BF16))
    gated = _sgu(zz, row(odd_ln_g[0]), row(odd_ln_b[0]), odd_w_s[0], odd_b_s[0].T)
    h = _proj_residual([gated], [odd_w_out[0].astype(BF16)], h)
    h = _ffn(h, row(ffn_norm[1]), ffn_w_gate[1].astype(BF16), ffn_w_up[1].astype(BF16),
             ffn_w_down[1].astype(BF16), row(final_norm), final_norm=True)
    return h.reshape(batch, seq, d)
```

```python
import functools

import jax
import jax.numpy as jnp
from jax import lax
from jax.experimental import pallas as pl
from jax.experimental.pallas import tpu as pltpu

F32 = jnp.float32
BF16 = jnp.bfloat16

D_MODEL = 2048
A_HEADS = 16
A_KV_HEADS = 2
A_HEAD_DIM = 64
WINDOW = 128
B_HEADS = 8
B_HEAD_DIM = 128
CONV_K = 4
DN_CHUNK = 64
DN_CHUNK_LOG2 = 6
C_GROUPS = 8
C_CHUNK = 128
C_GROUP_DIM = D_MODEL // C_GROUPS
EPS = 1e-6

LANES = 128
SUBLANES = 8
A_Q = A_HEADS * A_HEAD_DIM
A_KV = A_KV_HEADS * A_HEAD_DIM
B_W = B_HEADS * B_HEAD_DIM
EVEN_MAIN = A_Q + 2 * A_KV + 4 * B_W
BLK_KA = A_Q // LANES
BLK_VA = BLK_KA + A_KV // LANES
BLK_QB = BLK_VA + A_KV // LANES
BLK_KB = BLK_QB + B_HEADS
BLK_VB = BLK_KB + B_HEADS
BLK_Z = BLK_VB + B_HEADS
Z_SPLIT = 2 * LANES
GATE_BETA = 0
GATE_GAM = B_HEADS
GATE_LAST = 2 * B_HEADS
GATE_ROWS = 3 * B_HEADS

VMEM_LIMIT = 56 * 1024 * 1024


def _params(sem, vmem=VMEM_LIMIT):
    return pltpu.CompilerParams(dimension_semantics=sem, vmem_limit_bytes=vmem)


def _dot(a, b):
    return jnp.dot(a, b, preferred_element_type=F32)


def _dot_nt(a, b):
    return lax.dot_general(a, b, (((1,), (1,)), ((), ())), preferred_element_type=F32)


def _rms_rows(x, g):
    ms = jnp.mean(x * x, axis=-1, keepdims=True)
    return x * lax.rsqrt(ms + EPS) * g


def _silu(x):
    return x * jax.nn.sigmoid(x)


def _gelu(x):
    return x * (lax.erf(x * (2.0 ** -0.5)) + 1.0) * 0.5


def _norm_proj_kernel(x_ref, g_ref, w_ref, wg_ref, o_ref, og_ref, hn_ref):
    @pl.when(pl.program_id(1) == 0)
    def _():
        hn = _rms_rows(x_ref[...], g_ref[...]).astype(BF16)
        hn_ref[...] = hn
        og_ref[...] = _dot(hn, wg_ref[...])

    o_ref[...] = _dot(hn_ref[...], w_ref[...])


def _norm_proj_even(x, g, w_in3, w_gate, *, tm=1024, tn=1792):
    n, d = x.shape
    nout = EVEN_MAIN
    assert nout % tn == 0
    return pl.pallas_call(
        _norm_proj_kernel,
        out_shape=(jax.ShapeDtypeStruct((n, nout), F32),
                   jax.ShapeDtypeStruct((n, LANES), F32)),
        grid=(n // tm, nout // tn),
        in_specs=[pl.BlockSpec((tm, d), lambda i, j: (i, 0)),
                  pl.BlockSpec((1, d), lambda i, j: (0, 0)),
                  pl.BlockSpec((None, d, tn), lambda i, j: (0, 0, j)),
                  pl.BlockSpec((d, LANES), lambda i, j: (0, 0))],
        out_specs=(pl.BlockSpec((tm, tn), lambda i, j: (i, j)),
                   pl.BlockSpec((tm, LANES), lambda i, j: (i, 0))),
        scratch_shapes=[pltpu.VMEM((tm, d), BF16)],
        compiler_params=_params(("parallel", "arbitrary")),
        name="norm_proj_even",
    )(x, g, w_in3, w_gate)


def _proj_res_kernel(*refs, n_in):
    a_refs = refs[:n_in]
    w_refs = refs[n_in:2 * n_in]
    res_ref = refs[2 * n_in]
    o_ref = refs[2 * n_in + 1]
    acc = res_ref[...]
    for a_ref, w_ref in zip(a_refs, w_refs):
        acc = acc + _dot(a_ref[...], w_ref[...])
    o_ref[...] = acc


def _proj_residual(a_list, w3, res, *, tm=512):
    n, d = res.shape
    n_in = len(a_list)
    kk = a_list[0].shape[1]
    assert all(a.shape[1] == kk for a in a_list) and w3.shape[1] == n_in * kk
    in_specs = ([pl.BlockSpec((tm, kk), lambda i: (i, 0)) for _ in a_list]
                + [pl.BlockSpec((None, kk, d), lambda i, k=k: (0, k, 0)) for k in range(n_in)]
                + [pl.BlockSpec((tm, d), lambda i: (i, 0))])
    return pl.pallas_call(
        functools.partial(_proj_res_kernel, n_in=n_in),
        out_shape=jax.ShapeDtypeStruct((n, d), F32),
        grid=(n // tm,),
        in_specs=in_specs,
        out_specs=pl.BlockSpec((tm, d), lambda i: (i, 0)),
        compiler_params=_params(("parallel",)),
        name="proj_residual",
    )(*a_list, *([w3] * n_in), res)


def _ffn_kernel(x_ref, g_ref, wg_ref, wu_ref, wd_ref, fg_ref, o_ref, hn_ref, *, final_norm):
    f = pl.program_id(1)

    @pl.when(f == 0)
    def _():
        x = x_ref[...]
        hn_ref[...] = _rms_rows(x, g_ref[...]).astype(BF16)
        o_ref[...] = x

    hn = hn_ref[...]
    a = (_silu(_dot(hn, wg_ref[...])) * _dot(hn, wu_ref[...])).astype(BF16)
    o_ref[...] += _dot(a, wd_ref[...])

    if final_norm:
        @pl.when(f == pl.num_programs(1) - 1)
        def _():
            o_ref[...] = _rms_rows(o_ref[...], fg_ref[...])


def _ffn(x, g, w_gate, w_up, w_down, final_g, layer, *, final_norm, tm=1024, tf=512):
    n, d = x.shape
    dff = w_gate.shape[2]
    return pl.pallas_call(
        functools.partial(_ffn_kernel, final_norm=final_norm),
        out_shape=jax.ShapeDtypeStruct((n, d), F32),
        grid=(n // tm, dff // tf),
        in_specs=[pl.BlockSpec((tm, d), lambda i, f: (i, 0)),
                  pl.BlockSpec((1, d), lambda i, f: (0, 0)),
                  pl.BlockSpec((None, d, tf), lambda i, f: (layer, 0, f)),
                  pl.BlockSpec((None, d, tf), lambda i, f: (layer, 0, f)),
                  pl.BlockSpec((None, tf, d), lambda i, f: (layer, f, 0)),
                  pl.BlockSpec((1, d), lambda i, f: (0, 0))],
        out_specs=pl.BlockSpec((tm, d), lambda i, f: (i, 0)),
        scratch_shapes=[pltpu.VMEM((tm, d), BF16)],
        compiler_params=_params(("parallel", "arbitrary")),
        name="swiglu_ffn",
    )(x, g, w_gate, w_up, w_down, final_g)


def _swa_kernel(sink_ref, q_ref, k_ref, v_ref, kp_ref, vp_ref, o_ref):
    first = pl.program_id(1) == 0
    w = WINDOW
    grp = A_HEADS // A_KV_HEADS
    lane = lax.broadcasted_iota(jnp.int32, (w, LANES), 1)
    low = lane < A_HEAD_DIM
    row = lax.broadcasted_iota(jnp.int32, (w, w), 0)
    col = lax.broadcasted_iota(jnp.int32, (w, w), 1)
    cur_ok = jnp.concatenate([col <= row] * 2, axis=0)

    def dup_half(x, head):
        swapped = pltpu.roll(x, A_HEAD_DIM, 1)
        if head == 0:
            return jnp.where(low, x, swapped)
        return jnp.where(low, swapped, x)

    kc, vc, kp, vp = k_ref[...], v_ref[...], kp_ref[...], vp_ref[...]
    k2s = [jnp.concatenate([dup_half(kc, kvh), dup_half(kp, kvh)], axis=0).astype(BF16)
           for kvh in range(A_KV_HEADS)]
    v2s = [jnp.concatenate([dup_half(vc, kvh), dup_half(vp, kvh)], axis=0).astype(BF16)
           for kvh in range(A_KV_HEADS)]
    scale = A_HEAD_DIM ** -0.5
    zero = jnp.zeros((), BF16)
    pairs = range(A_HEADS // 2)
    s2s, sinks = [], []
    for pr in pairs:
        pair = q_ref[:, pr * LANES:(pr + 1) * LANES] * scale
        qst = jnp.concatenate([jnp.where(low, pair, 0.0), jnp.where(low, 0.0, pair)], axis=0)
        s2s.append(_dot_nt(qst.astype(BF16), k2s[2 * pr // grp]))
        sinks.append(jnp.concatenate([jnp.full((w, w), sink_ref[2 * pr], F32),
                                      jnp.full((w, w), sink_ref[2 * pr + 1], F32)], axis=0))
    p2s = []
    for s2, sink in zip(s2s, sinks):
        s = jnp.where(cur_ok, s2[:, :w], jnp.where(first, -jnp.inf, s2[:, w:]))
        m = jnp.maximum(jnp.max(s, axis=-1, keepdims=True), sink)
        p = jnp.exp(s - m)
        denom = jnp.sum(p, axis=-1, keepdims=True) + jnp.exp(sink - m)
        pb = (p * (1.0 / denom)).astype(BF16)
        p2s.append(jnp.concatenate([jnp.where(cur_ok, pb, zero), jnp.where(cur_ok, zero, pb)], axis=-1))
    for pr, p2 in zip(pairs, p2s):
        o = _dot(p2, v2s[2 * pr // grp])
        o_ref[:, pr * LANES:(pr + 1) * LANES] = jnp.where(low, o[:w], o[w:]).astype(o_ref.dtype)


def _swa(proj, sinks, batch, seq):
    nb = seq // WINDOW
    proj3 = proj.reshape(batch, seq, proj.shape[-1])
    spec = lambda blk, prev: pl.BlockSpec(
        (None, WINDOW, LANES),
        (lambda b, n: (b, jnp.maximum(n - 1, 0), blk)) if prev else (lambda b, n: (b, n, blk)))
    out = pl.pallas_call(
        _swa_kernel,
        out_shape=jax.ShapeDtypeStruct((batch, seq, A_Q), BF16),
        grid=(batch, nb),
        in_specs=[pl.BlockSpec(memory_space=pltpu.SMEM),
                  pl.BlockSpec((None, WINDOW, A_Q), lambda b, n: (b, n, 0)),
                  spec(BLK_KA, False), spec(BLK_VA, False),
                  spec(BLK_KA, True), spec(BLK_VA, True)],
        out_specs=pl.BlockSpec((None, WINDOW, A_Q), lambda b, n: (b, n, 0)),
        compiler_params=_params(("parallel", "arbitrary")),
        name="swa_sink_attention",
    )(sinks, proj3, proj3, proj3, proj3, proj3)
    return out.reshape(batch * seq, A_Q)


GATE_TB = 512
GDN_TB = 1024
GDN_NC = GDN_TB // DN_CHUNK
GDN_GRP = 128
GRP_NC = GDN_GRP // DN_CHUNK
SCAN_TB = 512
SCAN_NC = SCAN_TB // DN_CHUNK


def _split3(x):
    hi = x.astype(BF16)
    r1 = x - hi.astype(F32)
    mid = r1.astype(BF16)
    lo = (r1 - mid.astype(F32)).astype(BF16)
    return hi, mid, lo


def _gate_prep_kernel(gate_ref, alog_ref, dt_ref, o_ref, ot_ref):
    tb = GATE_TB
    gate = gate_ref[...]
    lane = lax.broadcasted_iota(jnp.int32, (tb, LANES), 1)
    beta = jax.nn.sigmoid(gate)
    g = -jnp.exp(alog_ref[...]) * jax.nn.softplus(gate + dt_ref[...])
    row = lax.broadcasted_iota(jnp.int32, (tb, tb), 0)
    col = lax.broadcasted_iota(jnp.int32, (tb, tb), 1)
    same = (row >> DN_CHUNK_LOG2) == (col >> DN_CHUNK_LOG2)
    tri = jnp.where(jnp.logical_and(same, row >= col), 1.0, 0.0).astype(BF16)
    ones = jnp.where(same, 1.0, 0.0).astype(BF16)
    r = _dot(jnp.concatenate([tri, ones], axis=0), jnp.concatenate(_split3(g), axis=-1))
    s = r[:, :LANES] + r[:, LANES:2 * LANES] + r[:, 2 * LANES:]
    out = jnp.where(lane < GATE_GAM, beta, jnp.where(lane < GATE_LAST, s[:tb], s[tb:]))
    o_ref[...] = out
    ot_ref[...] = out.T[:GATE_ROWS, :]


def _gate_prep(gates, alog_row, dt_row):
    n = gates.shape[0]
    tb = GATE_TB
    row = pl.BlockSpec((1, LANES), lambda i: (0, 0))
    return pl.pallas_call(
        _gate_prep_kernel,
        out_shape=(jax.ShapeDtypeStruct((n, LANES), F32),
                   jax.ShapeDtypeStruct((GATE_ROWS, n), F32)),
        grid=(n // tb,),
        in_specs=[pl.BlockSpec((tb, LANES), lambda i: (i, 0)), row, row],
        out_specs=(pl.BlockSpec((tb, LANES), lambda i: (i, 0)),
                   pl.BlockSpec((GATE_ROWS, tb), lambda i: (0, i))),
        compiler_params=_params(("parallel",)),
        name="gdn_gate_prep",
    )(gates, alog_row, dt_row)


def _conv_silu(x_ref, tail, w, buf_ref):
    t = x_ref.shape[0]
    x = x_ref[...]
    buf_ref[0:SUBLANES, :] = tail
    buf_ref[SUBLANES:, :] = x
    acc = x * w[CONV_K - 1:CONV_K, :]
    for j in range(CONV_K - 1):
        acc = acc + buf_ref[pl.ds(SUBLANES - (CONV_K - 1 - j), t), :] * w[j:j + 1, :]
    return _silu(acc)


def _lane_col(x, idx):
    lane = lax.broadcasted_iota(jnp.int32, x.shape, 1)
    return jnp.sum(jnp.where(lane == idx, x, 0.0), axis=-1, keepdims=True)


def _gdn_pre_kernel(q_ref, k_ref, v_ref, hq_ref, hk_ref, hv_ref, gc_ref, gr_ref,
                    cq_ref, ck_ref, cv_ref, kw_ref, nn_ref, qe_ref, ol_ref, gl_ref, buf_ref):
    h = pl.program_id(1)
    first = pl.program_id(2) == 0
    gs = GDN_GRP

    def conv(x_ref, halo_ref, w_ref, slot):
        tail = jnp.where(first, 0.0, halo_ref[...])
        return _conv_silu(x_ref, tail, w_ref[...], buf_ref.at[slot])

    q = conv(q_ref, hq_ref, cq_ref, 0)
    k = conv(k_ref, hk_ref, ck_ref, 1)
    v = conv(v_ref, hv_ref, cv_ref, 2)
    q = q * lax.rsqrt(jnp.sum(q * q, axis=-1, keepdims=True) + EPS) * (B_HEAD_DIM ** -0.5)
    k = k * lax.rsqrt(jnp.sum(k * k, axis=-1, keepdims=True) + EPS)

    gall = gc_ref[...]
    beta = _lane_col(gall, h + GATE_BETA)
    gam = _lane_col(gall, h + GATE_GAM)
    gam_end = _lane_col(gall, h + GATE_LAST)
    gam_row = gr_ref[pl.ds(h + GATE_GAM, 1), :]
    egam = jnp.exp(gam)
    q_dec = q * egam
    k_dec = k * jnp.exp(gam_end - gam)
    rhs_all = jnp.concatenate([v * beta, k * (beta * egam)], axis=-1)
    ends = gc_ref[pl.ds(DN_CHUNK - 1, GDN_NC, stride=DN_CHUNK), :]
    gl_ref[...] = jnp.broadcast_to(jnp.exp(_lane_col(ends, h + GATE_GAM)), (GDN_NC, LANES))

    row = lax.broadcasted_iota(jnp.int32, (gs, gs), 0)
    col = lax.broadcasted_iota(jnp.int32, (gs, gs), 1)
    same = (row >> DN_CHUNK_LOG2) == (col >> DN_CHUNK_LOG2)
    incl = jnp.logical_and(same, row >= col)
    strict = jnp.logical_and(same, row > col)
    trow = lax.broadcasted_iota(jnp.int32, (GRP_NC * B_HEAD_DIM, gs), 0)
    tcol = lax.broadcasted_iota(jnp.int32, (GRP_NC * B_HEAD_DIM, gs), 1)
    own = (trow >> 7) == (tcol >> DN_CHUNK_LOG2)

    groups = range(GDN_TB // gs)
    sls = [slice(grp * gs, (grp + 1) * gs) for grp in groups]
    kbs = [k[sl].astype(BF16) for sl in sls]
    scs = [_dot_nt(jnp.concatenate([q[sl].astype(BF16), kb], axis=0), kb)
           for sl, kb in zip(sls, kbs)]
    decays = [jnp.exp(jnp.where(incl, gam[sl] - gam_row[:, sl], -jnp.inf)) for sl in sls]
    qks = [(sc[:gs] * dec).astype(BF16) for sc, dec in zip(scs, decays)]
    xs = [jnp.where(strict, (-beta[sl]) * sc[gs:] * dec, 0.0)
          for sl, sc, dec in zip(sls, scs, decays)]
    ys = [rhs_all[sl] for sl in sls]
    for _ in range(DN_CHUNK_LOG2 - 1):
        xbs = [x.astype(BF16) for x in xs]
        rs = [_dot(xb, jnp.concatenate([xb, y.astype(BF16)], axis=-1))
              for xb, y in zip(xbs, ys)]
        xs = [r[:, :gs] for r in rs]
        ys = [y + r[:, gs:] for y, r in zip(ys, rs)]
    ys = [y + _dot(x.astype(BF16), y.astype(BF16)) for x, y in zip(xs, ys)]
    ybs = [y.astype(BF16) for y in ys]
    es = [_dot(qk, yb) for qk, yb in zip(qks, ybs)]
    kdt4s = [jnp.where(own, jnp.concatenate([k_dec[sl].T] * GRP_NC, axis=0), 0.0).astype(BF16)
             for sl in sls]
    fs = [_dot(kdt4, yb) for kdt4, yb in zip(kdt4s, ybs)]
    for grp, sl, e, f in zip(groups, sls, es, fs):
        ol_ref[sl, :] = e[:, :B_HEAD_DIM]
        qe_ref[sl, :] = (q_dec[sl] - e[:, B_HEAD_DIM:]).astype(qe_ref.dtype)
        rws = slice(grp * GRP_NC * B_HEAD_DIM, (grp + 1) * GRP_NC * B_HEAD_DIM)
        nn_ref[rws, :] = f[:, :B_HEAD_DIM]
        kw_ref[rws, :] = f[:, B_HEAD_DIM:].astype(kw_ref.dtype)


def _gdn_scan_kernel(kw_ref, nn_ref, qe_ref, ol_ref, gl_ref, *rest):
    nz = B_W // Z_SPLIT
    z_refs = rest[:nz]
    onorm_ref, o_ref, state_ref = rest[nz:]
    c = DN_CHUNK
    dk = B_HEAD_DIM
    hpz = Z_SPLIT // dk

    @pl.when(pl.program_id(1) == 0)
    def _():
        state_ref[...] = jnp.zeros_like(state_ref)

    onorm = onorm_ref[...]
    for n in range(SCAN_NC):
        ts = slice(n * c, (n + 1) * c)
        ds = slice(n * dk, (n + 1) * dk)
        for h in range(B_HEADS):
            hs = slice(h * dk, (h + 1) * dk)
            s = state_ref[h]
            r = _dot(jnp.concatenate([kw_ref[h, ds, :], qe_ref[h, ts, :]], axis=0), s.astype(BF16))
            state_ref[h] = s * gl_ref[h, n:n + 1, :] + nn_ref[h, ds, :] - r[:dk]
            o = ol_ref[h, ts, :] + r[dk:]
            o = o * lax.rsqrt(jnp.mean(o * o, axis=-1, keepdims=True) + EPS) * onorm
            z = z_refs[h // hpz][ts, (h % hpz) * dk:(h % hpz + 1) * dk]
            o_ref[ts, hs] = (o * _silu(z)).astype(o_ref.dtype)


def _gdn(proj, gates, conv_w, alog_row, dt_row, onorm, batch, seq):
    gcol, grow = _gate_prep(gates, alog_row, dt_row)
    proj3 = proj.reshape(batch, seq, proj.shape[-1])
    gcol3 = gcol.reshape(batch, seq, LANES)
    tb = GDN_TB
    nt = seq // tb
    nch = seq // DN_CHUNK
    dk = B_HEAD_DIM
    tok = lambda blk: pl.BlockSpec((None, tb, LANES), lambda b, h, t: (b, t, blk + h))
    halo = lambda blk: pl.BlockSpec(
        (None, SUBLANES, LANES),
        lambda b, h, t: (b, jnp.maximum(t * (tb // SUBLANES) - 1, 0), blk + h))
    cw = lambda blk: pl.BlockSpec((CONV_K, LANES), lambda b, h, t: (0, blk + h))
    per_chunk = lambda rows: pl.BlockSpec((None, None, rows, LANES), lambda b, h, t: (b, h, t, 0))
    kw, nn, qe, ol, gl = pl.pallas_call(
        _gdn_pre_kernel,
        out_shape=(jax.ShapeDtypeStruct((batch, B_HEADS, nch * dk, dk), BF16),
                   jax.ShapeDtypeStruct((batch, B_HEADS, nch * dk, dk), F32),
                   jax.ShapeDtypeStruct((batch, B_HEADS, seq, dk), BF16),
                   jax.ShapeDtypeStruct((batch, B_HEADS, seq, dk), F32),
                   jax.ShapeDtypeStruct((batch, B_HEADS, nch, LANES), F32)),
        grid=(batch, B_HEADS, nt),
        in_specs=[tok(BLK_QB), tok(BLK_KB), tok(BLK_VB),
                  halo(BLK_QB), halo(BLK_KB), halo(BLK_VB),
                  pl.BlockSpec((None, tb, LANES), lambda b, h, t: (b, t, 0)),
                  pl.BlockSpec((GATE_ROWS, tb), lambda b, h, t: (0, b * nt + t)),
                  cw(0), cw(B_HEADS), cw(2 * B_HEADS)],
        out_specs=(per_chunk(GDN_NC * dk), per_chunk(GDN_NC * dk), per_chunk(tb), per_chunk(tb),
                   per_chunk(GDN_NC)),
        scratch_shapes=[pltpu.VMEM((3, SUBLANES + tb, LANES), F32)],
        compiler_params=_params(("parallel", "parallel", "parallel")),
        name="gdn_pre",
    )(proj3, proj3, proj3, proj3, proj3, proj3, gcol3, grow, conv_w, conv_w, conv_w)

    heads = lambda rows: pl.BlockSpec((None, B_HEADS, rows, LANES), lambda b, t: (b, 0, t, 0))
    ts = SCAN_TB
    nz = B_W // Z_SPLIT
    assert (BLK_Z * LANES) % Z_SPLIT == 0
    z_blk0 = BLK_Z * LANES // Z_SPLIT
    out = pl.pallas_call(
        _gdn_scan_kernel,
        out_shape=jax.ShapeDtypeStruct((batch, seq, B_W), BF16),
        grid=(batch, seq // ts),
        in_specs=[heads(SCAN_NC * dk), heads(SCAN_NC * dk), heads(ts), heads(ts), heads(SCAN_NC)]
        + [pl.BlockSpec((None, ts, Z_SPLIT), lambda b, t, k=k: (b, t, z_blk0 + k)) for k in range(nz)]
        + [pl.BlockSpec((1, dk), lambda b, t: (0, 0))],
        out_specs=pl.BlockSpec((None, ts, B_W), lambda b, t: (b, t, 0)),
        scratch_shapes=[pltpu.VMEM((B_HEADS, dk, dk), F32)],
        compiler_params=_params(("parallel", "arbitrary")),
        name="gdn_scan",
    )(kw, nn, qe, ol, gl, *([proj3] * nz), onorm)
    return out.reshape(batch * seq, B_W)


ODD_TM = 512
ODD_TN = 1024
ODD_NV = D_MODEL // ODD_TN


def _odd_mixer_kernel(x_ref, g_ref, win_ref, lg_ref, lb_ref, ws_ref, bs_ref, wout_ref,
                      o_ref, hn_ref, v_ref, y_ref, gated_ref):
    j = pl.program_id(1)
    cc = C_CHUNK
    gpb = ODD_TN // C_GROUP_DIM

    @pl.when(j == 0)
    def _():
        hn_ref[...] = _rms_rows(x_ref[...], g_ref[...]).astype(BF16)

    @pl.when(j < ODD_NV)
    def _():
        v_ref[j] = _gelu(_dot(hn_ref[...], win_ref[...]))

    for blk in range(ODD_NV):
        @pl.when(j == ODD_NV + blk)
        def _(blk=blk):
            if blk == 0:
                v = jnp.concatenate([v_ref[k] for k in range(ODD_NV)], axis=-1)
                mu = jnp.mean(v, axis=-1, keepdims=True)
                vc = v - mu
                var = jnp.mean(vc * vc, axis=-1, keepdims=True)
                y_ref[...] = (vc * lax.rsqrt(var + EPS) * lg_ref[...] + lb_ref[...]).astype(BF16)
            u = _gelu(_dot(hn_ref[...], win_ref[...]))
            row = lax.broadcasted_iota(jnp.int32, (cc, cc), 0)
            col = lax.broadcasted_iota(jnp.int32, (cc, cc), 1)
            bs = bs_ref[...]
            for gl in range(gpb):
                gi = blk * gpb + gl
                ws = jnp.where(row >= col, ws_ref[gi], 0.0).astype(BF16)
                bias = bs[:, gi:gi + 1]
                cs = slice(gi * C_GROUP_DIM, (gi + 1) * C_GROUP_DIM)
                us = slice(gl * C_GROUP_DIM, (gl + 1) * C_GROUP_DIM)
                for n in range(ODD_TM // cc):
                    rs = slice(n * cc, (n + 1) * cc)
                    mixed = _dot(ws, y_ref[rs, cs]) + bias
                    gated_ref[rs, cs] = (u[rs, us] * mixed).astype(BF16)

    @pl.when(j == 2 * ODD_NV)
    def _():
        o_ref[...] = x_ref[...] + _dot(gated_ref[...], wout_ref[...])


def _odd_mixer(x, g, w_in3, ln_g, ln_b, w_s, b_s_t, w_out3):
    n, d = x.shape
    tm, tn, nv = ODD_TM, ODD_TN, ODD_NV
    win_blk = lambda i, j: (0, 0, jnp.where(j < nv, j + nv, jnp.minimum(j - nv, nv - 1)))
    const = lambda *shape: pl.BlockSpec(shape, lambda i, j: (0,) * len(shape))
    return pl.pallas_call(
        _odd_mixer_kernel,
        out_shape=jax.ShapeDtypeStruct((n, d), F32),
        grid=(n // tm, 2 * nv + 1),
        in_specs=[pl.BlockSpec((tm, d), lambda i, j: (i, 0)),
                  const(1, d),
                  pl.BlockSpec((None, d, tn), win_blk),
                  const(1, d), const(1, d),
                  const(C_GROUPS, C_CHUNK, C_CHUNK),
                  const(C_CHUNK, C_GROUPS),
                  pl.BlockSpec((None, d, d), lambda i, j: (0, 0, 0))],
        out_specs=pl.BlockSpec((tm, d), lambda i, j: (i, 0)),
        scratch_shapes=[pltpu.VMEM((tm, d), BF16),
                        pltpu.VMEM((nv, tm, tn), F32),
                        pltpu.VMEM((tm, d), BF16),
                        pltpu.VMEM((tm, d), BF16)],
        compiler_params=_params(("parallel", "arbitrary")),
        name="odd_mixer",
    )(x, g, w_in3, ln_g, ln_b, w_s, b_s_t, w_out3)


def kernel(x, even_norm, even_w_in, even_conv, even_a_log, even_dt_bias, even_sinks,
           even_onorm, even_w_out, odd_norm, odd_w_in, odd_ln_g, odd_ln_b, odd_w_s,
           odd_b_s, odd_w_out, ffn_norm, ffn_w_gate, ffn_w_up, ffn_w_down, final_norm):
    batch, seq, d = x.shape
    n = batch * seq
    h = x.reshape(n, d)
    row = lambda a: a.reshape(1, -1).astype(F32)

    w_bg = even_w_in[0, :, EVEN_MAIN:]
    w_gate = jnp.pad(jnp.concatenate([w_bg, w_bg[:, B_HEADS:]], axis=1),
                     ((0, 0), (0, LANES - GATE_ROWS))).astype(BF16)
    proj, gates = _norm_proj_even(h, row(even_norm[0]), even_w_in.astype(BF16), w_gate)
    out_a = _swa(proj, even_sinks[0].astype(F32), batch, seq)

    def gate_row(a):
        a = a.astype(F32)
        return jnp.pad(jnp.concatenate([a, a]), (GATE_GAM, LANES - GATE_ROWS)).reshape(1, LANES)

    out_b = _gdn(proj, gates, even_conv[0], gate_row(even_a_log[0]), gate_row(even_dt_bias[0]),
                 row(even_onorm[0]), batch, seq)
    h = _proj_residual([out_a, out_b], even_w_out.astype(BF16), h)
    wg, wu, wd = ffn_w_gate.astype(BF16), ffn_w_up.astype(BF16), ffn_w_down.astype(BF16)
    h = _ffn(h, row(ffn_norm[0]), wg, wu, wd, row(final_norm), 0, final_norm=False)

    h = _odd_mixer(h, row(odd_norm[0]), odd_w_in.astype(BF16), row(odd_ln_g[0]), row(odd_ln_b[0]),
                   odd_w_s[0], odd_b_s[0].T, odd_w_out.astype(BF16))
    h = _ffn(h, row(ffn_norm[1]), wg, wu, wd, row(final_norm), 1, final_norm=True)
    return h.reshape(batch, seq, d)
```

```python
import functools

import jax
import jax.numpy as jnp
from jax import lax
from jax.experimental import pallas as pl
from jax.experimental.pallas import tpu as pltpu

F32 = jnp.float32
BF16 = jnp.bfloat16

D_MODEL = 2048
A_HEADS = 16
A_KV_HEADS = 2
A_HEAD_DIM = 64
WINDOW = 128
B_HEADS = 8
B_HEAD_DIM = 128
CONV_K = 4
DN_CHUNK = 64
DN_CHUNK_LOG2 = 6
C_GROUPS = 8
C_CHUNK = 128
C_GROUP_DIM = D_MODEL // C_GROUPS
EPS = 1e-6

LANES = 128
SUBLANES = 8
A_Q = A_HEADS * A_HEAD_DIM
A_KV = A_KV_HEADS * A_HEAD_DIM
B_W = B_HEADS * B_HEAD_DIM
EVEN_MAIN = A_Q + 2 * A_KV + 4 * B_W
BLK_KA = A_Q // LANES
BLK_VA = BLK_KA + A_KV // LANES
BLK_QB = BLK_VA + A_KV // LANES
BLK_KB = BLK_QB + B_HEADS
BLK_VB = BLK_KB + B_HEADS
BLK_Z = BLK_VB + B_HEADS
Z_SPLIT = 2 * LANES
GATE_BETA = 0
GATE_GAM = B_HEADS
GATE_LAST = 2 * B_HEADS
GATE_ROWS = 3 * B_HEADS

MIB = 1024 * 1024


def _params(sem, vmem_mib):
    return pltpu.CompilerParams(dimension_semantics=sem, vmem_limit_bytes=vmem_mib * MIB)


def _dot(a, b):
    return jnp.dot(a, b, preferred_element_type=F32)


def _dot_nt(a, b):
    return lax.dot_general(a, b, (((1,), (1,)), ((), ())), preferred_element_type=F32)


def _rms_rows(x, g):
    ms = jnp.mean(x * x, axis=-1, keepdims=True)
    return x * lax.rsqrt(ms + EPS) * g


def _silu(x):
    return x * jax.nn.sigmoid(x)


def _gelu(x):
    return x * (lax.erf(x * (2.0 ** -0.5)) + 1.0) * 0.5


def _norm_proj_kernel(x_ref, g_ref, w_ref, wg_ref, o_ref, og_ref, hn_ref):
    @pl.when(pl.program_id(1) == 0)
    def _():
        hn = _rms_rows(x_ref[...], g_ref[...]).astype(BF16)
        hn_ref[...] = hn
        og_ref[...] = _dot(hn, wg_ref[...])

    o_ref[...] = _dot(hn_ref[...], w_ref[...])


def _norm_proj_even(x, g, w_in3, w_gate, *, tm=1024, tn=1792):
    n, d = x.shape
    nout = EVEN_MAIN
    assert nout % tn == 0
    return pl.pallas_call(
        _norm_proj_kernel,
        out_shape=(jax.ShapeDtypeStruct((n, nout), F32),
                   jax.ShapeDtypeStruct((n, LANES), F32)),
        grid=(n // tm, nout // tn),
        in_specs=[pl.BlockSpec((tm, d), lambda i, j: (i, 0)),
                  pl.BlockSpec((1, d), lambda i, j: (0, 0)),
                  pl.BlockSpec((None, d, tn), lambda i, j: (0, 0, j)),
                  pl.BlockSpec((d, LANES), lambda i, j: (0, 0))],
        out_specs=(pl.BlockSpec((tm, tn), lambda i, j: (i, j)),
                   pl.BlockSpec((tm, LANES), lambda i, j: (i, 0))),
        scratch_shapes=[pltpu.VMEM((tm, d), BF16)],
        compiler_params=_params(("parallel", "arbitrary"), 56),
        name="norm_proj_even",
    )(x, g, w_in3, w_gate)


def _proj_res_kernel(*refs, n_in):
    a_refs = refs[:n_in]
    w_refs = refs[n_in:2 * n_in]
    res_ref = refs[2 * n_in]
    o_ref = refs[2 * n_in + 1]
    acc = res_ref[...]
    for a_ref, w_ref in zip(a_refs, w_refs):
        acc = acc + _dot(a_ref[...], w_ref[...])
    o_ref[...] = acc


def _proj_residual(a_list, w3, res, *, tm=512):
    n, d = res.shape
    n_in = len(a_list)
    kk = a_list[0].shape[1]
    assert all(a.shape[1] == kk for a in a_list) and w3.shape[1] == n_in * kk
    in_specs = ([pl.BlockSpec((tm, kk), lambda i: (i, 0)) for _ in a_list]
                + [pl.BlockSpec((None, kk, d), lambda i, k=k: (0, k, 0)) for k in range(n_in)]
                + [pl.BlockSpec((tm, d), lambda i: (i, 0))])
    return pl.pallas_call(
        functools.partial(_proj_res_kernel, n_in=n_in),
        out_shape=jax.ShapeDtypeStruct((n, d), F32),
        grid=(n // tm,),
        in_specs=in_specs,
        out_specs=pl.BlockSpec((tm, d), lambda i: (i, 0)),
        compiler_params=_params(("parallel",), 34),
        name="proj_residual",
    )(*a_list, *([w3] * n_in), res)


def _ffn_kernel(x_ref, g_ref, wg_ref, wu_ref, wd_ref, fg_ref, o_ref, hn_ref, *, final_norm):
    f = pl.program_id(1)

    @pl.when(f == 0)
    def _():
        x = x_ref[...]
        hn_ref[...] = _rms_rows(x, g_ref[...]).astype(BF16)
        o_ref[...] = x

    hn = hn_ref[...]
    a = (_silu(_dot(hn, wg_ref[...])) * _dot(hn, wu_ref[...])).astype(BF16)
    o_ref[...] += _dot(a, wd_ref[...])

    if final_norm:
        @pl.when(f == pl.num_programs(1) - 1)
        def _():
            o_ref[...] = _rms_rows(o_ref[...], fg_ref[...])


def _ffn(x, g, w_gate, w_up, w_down, final_g, *, final_norm, tm=1024, tf=512):
    n, d = x.shape
    dff = w_gate.shape[1]
    return pl.pallas_call(
        functools.partial(_ffn_kernel, final_norm=final_norm),
        out_shape=jax.ShapeDtypeStruct((n, d), F32),
        grid=(n // tm, dff // tf),
        in_specs=[pl.BlockSpec((tm, d), lambda i, f: (i, 0)),
                  pl.BlockSpec((1, d), lambda i, f: (0, 0)),
                  pl.BlockSpec((d, tf), lambda i, f: (0, f)),
                  pl.BlockSpec((d, tf), lambda i, f: (0, f)),
                  pl.BlockSpec((tf, d), lambda i, f: (f, 0)),
                  pl.BlockSpec((1, d), lambda i, f: (0, 0))],
        out_specs=pl.BlockSpec((tm, d), lambda i, f: (i, 0)),
        scratch_shapes=[pltpu.VMEM((tm, d), BF16)],
        compiler_params=_params(("parallel", "arbitrary"), 56),
        name="swiglu_ffn",
    )(x, g, w_gate, w_up, w_down, final_g)


SWA_SUB = 2

def _swa_kernel(sink_ref, q_ref, k_ref, v_ref, kp_ref, vp_ref, o_ref):
    first = pl.program_id(1) == 0
    w = WINDOW
    grp = A_HEADS // A_KV_HEADS
    lane = lax.broadcasted_iota(jnp.int32, (w, LANES), 1)
    low = lane < A_HEAD_DIM
    row = lax.broadcasted_iota(jnp.int32, (w, w), 0)
    col = lax.broadcasted_iota(jnp.int32, (w, w), 1)
    cur_ok = jnp.concatenate([col <= row] * 2, axis=0)

    def dup_half(x, head):
        swapped = pltpu.roll(x, A_HEAD_DIM, 1)
        if head == 0:
            return jnp.where(low, x, swapped)
        return jnp.where(low, swapped, x)

    kvs = [(kp_ref[...], vp_ref[...])] + [(k_ref[sb * w:(sb + 1) * w, :], v_ref[sb * w:(sb + 1) * w, :])
                                           for sb in range(SWA_SUB)]
    dups = [[(dup_half(kk, kvh), dup_half(vv, kvh)) for kk, vv in kvs] for kvh in range(A_KV_HEADS)]
    k2s = [[jnp.concatenate([d[sb + 1][0], d[sb][0]], axis=0).astype(BF16) for sb in range(SWA_SUB)]
           for d in dups]
    v2s = [[jnp.concatenate([d[sb + 1][1], d[sb][1]], axis=0).astype(BF16) for sb in range(SWA_SUB)]
           for d in dups]
    scale = A_HEAD_DIM ** -0.5
    zero = jnp.zeros((), BF16)
    chains = [(sb, pr) for sb in range(SWA_SUB) for pr in range(A_HEADS // 2)]
    s2s, sinks = [], []
    for sb, pr in chains:
        pair = q_ref[sb * w:(sb + 1) * w, pr * LANES:(pr + 1) * LANES] * scale
        qst = jnp.concatenate([jnp.where(low, pair, 0.0), jnp.where(low, 0.0, pair)], axis=0)
        s2s.append(_dot_nt(qst.astype(BF16), k2s[2 * pr // grp][sb]))
        sinks.append(jnp.concatenate([jnp.full((w, w), sink_ref[2 * pr], F32),
                                      jnp.full((w, w), sink_ref[2 * pr + 1], F32)], axis=0))
    p2s = []
    for (sb, pr), s2, sink in zip(chains, s2s, sinks):
        prev = jnp.where(first, -jnp.inf, s2[:, w:]) if sb == 0 else s2[:, w:]
        s = jnp.where(cur_ok, s2[:, :w], prev)
        m = jnp.maximum(jnp.max(s, axis=-1, keepdims=True), sink)
        p = jnp.exp(s - m)
        denom = jnp.sum(p, axis=-1, keepdims=True) + jnp.exp(sink - m)
        pb = (p * (1.0 / denom)).astype(BF16)
        p2s.append(jnp.concatenate([jnp.where(cur_ok, pb, zero), jnp.where(cur_ok, zero, pb)], axis=-1))
    for (sb, pr), p2 in zip(chains, p2s):
        o = _dot(p2, v2s[2 * pr // grp][sb])
        o_ref[sb * w:(sb + 1) * w, pr * LANES:(pr + 1) * LANES] = (
            jnp.where(low, o[:w], o[w:]).astype(o_ref.dtype))


def _swa(proj, sinks, batch, seq):
    tq = SWA_SUB * WINDOW
    nb = seq // tq
    proj3 = proj.reshape(batch, seq, proj.shape[-1])
    spec = lambda blk, prev: (
        pl.BlockSpec((None, WINDOW, LANES), lambda b, n: (b, jnp.maximum(n * SWA_SUB - 1, 0), blk))
        if prev else pl.BlockSpec((None, tq, LANES), lambda b, n: (b, n, blk)))
    out = pl.pallas_call(
        _swa_kernel,
        out_shape=jax.ShapeDtypeStruct((batch, seq, A_Q), BF16),
        grid=(batch, nb),
        in_specs=[pl.BlockSpec(memory_space=pltpu.SMEM),
                  pl.BlockSpec((None, tq, A_Q), lambda b, n: (b, n, 0)),
                  spec(BLK_KA, False), spec(BLK_VA, False),
                  spec(BLK_KA, True), spec(BLK_VA, True)],
        out_specs=pl.BlockSpec((None, tq, A_Q), lambda b, n: (b, n, 0)),
        compiler_params=_params(("parallel", "arbitrary"), 16),
        name="swa_sink_attention",
    )(sinks, proj3, proj3, proj3, proj3, proj3)
    return out.reshape(batch * seq, A_Q)


GATE_TB = 512
GDN_TB = 1024
GDN_NC = GDN_TB // DN_CHUNK
GDN_GRP = 128
GRP_NC = GDN_GRP // DN_CHUNK
SCAN_TB = 512
SCAN_NC = SCAN_TB // DN_CHUNK


def _split3(x):
    hi = x.astype(BF16)
    r1 = x - hi.astype(F32)
    mid = r1.astype(BF16)
    lo = (r1 - mid.astype(F32)).astype(BF16)
    return hi, mid, lo


def _gate_prep_kernel(gate_ref, alog_ref, dt_ref, o_ref, ot_ref):
    tb = GATE_TB
    gate = gate_ref[...]
    lane = lax.broadcasted_iota(jnp.int32, (tb, LANES), 1)
    beta = jax.nn.sigmoid(gate)
    g = -jnp.exp(alog_ref[...]) * jax.nn.softplus(gate + dt_ref[...])
    row = lax.broadcasted_iota(jnp.int32, (tb, tb), 0)
    col = lax.broadcasted_iota(jnp.int32, (tb, tb), 1)
    same = (row >> DN_CHUNK_LOG2) == (col >> DN_CHUNK_LOG2)
    tri = jnp.where(jnp.logical_and(same, row >= col), 1.0, 0.0).astype(BF16)
    ones = jnp.where(same, 1.0, 0.0).astype(BF16)
    r = _dot(jnp.concatenate([tri, ones], axis=0), jnp.concatenate(_split3(g), axis=-1))
    s = r[:, :LANES] + r[:, LANES:2 * LANES] + r[:, 2 * LANES:]
    out = jnp.where(lane < GATE_GAM, beta, jnp.where(lane < GATE_LAST, s[:tb], s[tb:]))
    o_ref[...] = out
    ot_ref[...] = out.T[:GATE_ROWS, :]


def _gate_prep(gates, alog_row, dt_row):
    n = gates.shape[0]
    tb = GATE_TB
    row = pl.BlockSpec((1, LANES), lambda i: (0, 0))
    return pl.pallas_call(
        _gate_prep_kernel,
        out_shape=(jax.ShapeDtypeStruct((n, LANES), F32),
                   jax.ShapeDtypeStruct((GATE_ROWS, n), F32)),
        grid=(n // tb,),
        in_specs=[pl.BlockSpec((tb, LANES), lambda i: (i, 0)), row, row],
        out_specs=(pl.BlockSpec((tb, LANES), lambda i: (i, 0)),
                   pl.BlockSpec((GATE_ROWS, tb), lambda i: (0, i))),
        compiler_params=_params(("parallel",), 8),
        name="gdn_gate_prep",
    )(gates, alog_row, dt_row)


def _conv_silu(x_ref, tail, w, buf_ref):
    t = x_ref.shape[0]
    x = x_ref[...]
    buf_ref[0:SUBLANES, :] = tail
    buf_ref[SUBLANES:, :] = x
    acc = x * w[CONV_K - 1:CONV_K, :]
    for j in range(CONV_K - 1):
        acc = acc + buf_ref[pl.ds(SUBLANES - (CONV_K - 1 - j), t), :] * w[j:j + 1, :]
    return _silu(acc)


def _lane_col(x, idx):
    lane = lax.broadcasted_iota(jnp.int32, x.shape, 1)
    return jnp.sum(jnp.where(lane == idx, x, 0.0), axis=-1, keepdims=True)


def _gdn_pre_kernel(q_ref, k_ref, v_ref, hq_ref, hk_ref, hv_ref, gc_ref, gr_ref,
                    cq_ref, ck_ref, cv_ref, *rest, n_cast, n_cast_step):
    cast_in = rest[:n_cast]
    kw_ref, nn_ref, qe_ref, ol_ref, gl_ref = rest[n_cast:n_cast + 5]
    cast_out = rest[n_cast + 5:2 * n_cast + 5]
    buf_ref = rest[2 * n_cast + 5]
    h = pl.program_id(1)
    first = pl.program_id(2) == 0
    gs = GDN_GRP

    for src, dst in zip(cast_in[:n_cast_step], cast_out[:n_cast_step]):
        dst[...] = src[...].astype(dst.dtype)

    @pl.when(first)
    def _():
        for src, dst in zip(cast_in[n_cast_step:], cast_out[n_cast_step:]):
            dst[...] = src[...].astype(dst.dtype)

    def conv(x_ref, halo_ref, w_ref, slot):
        tail = jnp.where(first, 0.0, halo_ref[...])
        return _conv_silu(x_ref, tail, w_ref[...], buf_ref.at[slot])

    q = conv(q_ref, hq_ref, cq_ref, 0)
    k = conv(k_ref, hk_ref, ck_ref, 1)
    v = conv(v_ref, hv_ref, cv_ref, 2)
    q = q * lax.rsqrt(jnp.sum(q * q, axis=-1, keepdims=True) + EPS) * (B_HEAD_DIM ** -0.5)
    k = k * lax.rsqrt(jnp.sum(k * k, axis=-1, keepdims=True) + EPS)

    gall = gc_ref[...]
    beta = _lane_col(gall, h + GATE_BETA)
    gam = _lane_col(gall, h + GATE_GAM)
    gam_end = _lane_col(gall, h + GATE_LAST)
    gam_row = gr_ref[pl.ds(h + GATE_GAM, 1), :]
    egam = jnp.exp(gam)
    q_dec = q * egam
    k_dec = k * jnp.exp(gam_end - gam)
    rhs_all = jnp.concatenate([v * beta, k * (beta * egam)], axis=-1)
    ends = gc_ref[pl.ds(DN_CHUNK - 1, GDN_NC, stride=DN_CHUNK), :]
    gl_ref[...] = jnp.broadcast_to(jnp.exp(_lane_col(ends, h + GATE_GAM)), (GDN_NC, LANES))

    row = lax.broadcasted_iota(jnp.int32, (gs, gs), 0)
    col = lax.broadcasted_iota(jnp.int32, (gs, gs), 1)
    same = (row >> DN_CHUNK_LOG2) == (col >> DN_CHUNK_LOG2)
    incl = jnp.logical_and(same, row >= col)
    strict = jnp.logical_and(same, row > col)
    trow = lax.broadcasted_iota(jnp.int32, (GRP_NC * B_HEAD_DIM, gs), 0)
    tcol = lax.broadcasted_iota(jnp.int32, (GRP_NC * B_HEAD_DIM, gs), 1)
    own = (trow >> 7) == (tcol >> DN_CHUNK_LOG2)

    groups = range(GDN_TB // gs)
    sls = [slice(grp * gs, (grp + 1) * gs) for grp in groups]
    kbs = [k[sl].astype(BF16) for sl in sls]
    scs = [_dot_nt(jnp.concatenate([q[sl].astype(BF16), kb], axis=0), kb)
           for sl, kb in zip(sls, kbs)]
    decays = [jnp.exp(jnp.where(incl, gam[sl] - gam_row[:, sl], -jnp.inf)) for sl in sls]
    qks = [(sc[:gs] * dec).astype(BF16) for sc, dec in zip(scs, decays)]
    xs = [jnp.where(strict, (-beta[sl]) * sc[gs:] * dec, 0.0)
          for sl, sc, dec in zip(sls, scs, decays)]
    ys = [rhs_all[sl] for sl in sls]
    for _ in range(DN_CHUNK_LOG2 - 1):
        xbs = [x.astype(BF16) for x in xs]
        rs = [_dot(xb, jnp.concatenate([xb, y.astype(BF16)], axis=-1))
              for xb, y in zip(xbs, ys)]
        xs = [r[:, :gs] for r in rs]
        ys = [y + r[:, gs:] for y, r in zip(ys, rs)]
    ys = [y + _dot(x.astype(BF16), y.astype(BF16)) for x, y in zip(xs, ys)]
    ybs = [y.astype(BF16) for y in ys]
    es = [_dot(qk, yb) for qk, yb in zip(qks, ybs)]
    kdt4s = [jnp.where(own, jnp.concatenate([k_dec[sl].T] * GRP_NC, axis=0), 0.0).astype(BF16)
             for sl in sls]
    fs = [_dot(kdt4, yb) for kdt4, yb in zip(kdt4s, ybs)]
    for grp, sl, e, f in zip(groups, sls, es, fs):
        ol_ref[sl, :] = e[:, :B_HEAD_DIM]
        qe_ref[sl, :] = (q_dec[sl] - e[:, B_HEAD_DIM:]).astype(qe_ref.dtype)
        rws = slice(grp * GRP_NC * B_HEAD_DIM, (grp + 1) * GRP_NC * B_HEAD_DIM)
        nn_ref[rws, :] = f[:, :B_HEAD_DIM]
        kw_ref[rws, :] = f[:, B_HEAD_DIM:].astype(kw_ref.dtype)


def _gdn_scan_kernel(kw_ref, nn_ref, qe_ref, ol_ref, gl_ref, *rest):
    nz = B_W // Z_SPLIT
    z_refs = rest[:nz]
    onorm_ref, o_ref, state_ref = rest[nz:]
    c = DN_CHUNK
    dk = B_HEAD_DIM
    hpz = Z_SPLIT // dk

    @pl.when(pl.program_id(1) == 0)
    def _():
        state_ref[...] = jnp.zeros_like(state_ref)

    onorm = onorm_ref[...]
    for n in range(SCAN_NC):
        ts = slice(n * c, (n + 1) * c)
        ds = slice(n * dk, (n + 1) * dk)
        for h in range(B_HEADS):
            hs = slice(h * dk, (h + 1) * dk)
            s = state_ref[h]
            r = _dot(jnp.concatenate([kw_ref[h, ds, :], qe_ref[h, ts, :]], axis=0), s.astype(BF16))
            state_ref[h] = s * gl_ref[h, n:n + 1, :] + nn_ref[h, ds, :] - r[:dk]
            o = ol_ref[h, ts, :] + r[dk:]
            o = o * lax.rsqrt(jnp.mean(o * o, axis=-1, keepdims=True) + EPS) * onorm
            z = z_refs[h // hpz][ts, (h % hpz) * dk:(h % hpz + 1) * dk]
            o_ref[ts, hs] = (o * _silu(z)).astype(o_ref.dtype)


def _gdn(proj, gates, conv_w, alog_row, dt_row, onorm, batch, seq, cast_list):
    gcol, grow = _gate_prep(gates, alog_row, dt_row)
    tb = GDN_TB
    nt = seq // tb
    bf16_rows = 2 * SUBLANES
    n_steps = batch * B_HEADS * nt
    per_step = lambda wl: wl[0].shape[1] % (n_steps * bf16_rows) == 0
    order = sorted(range(len(cast_list)), key=lambda idx: not per_step(cast_list[idx]))
    cast_list = [cast_list[idx] for idx in order]
    n_cast_step = sum(per_step(wl) for wl in cast_list)
    cast_in_specs, cast_out_specs, cast_shapes = [], [], []
    for w, layer in cast_list:
        if per_step((w, layer)):
            n_slab, slab = n_steps, (lambda b, h, t: (b * B_HEADS + h) * nt + t)
        else:
            n_slab, slab = batch * B_HEADS, (lambda b, h, t: b * B_HEADS + h)
        rows, cols = w.shape[1] // n_slab, w.shape[2]
        assert w.shape[1] % n_slab == 0 and rows % bf16_rows == 0
        cast_in_specs.append(pl.BlockSpec((None, rows, cols),
                                          lambda b, h, t, layer=layer, slab=slab: (layer, slab(b, h, t), 0)))
        cast_out_specs.append(pl.BlockSpec((rows, cols), lambda b, h, t, slab=slab: (slab(b, h, t), 0)))
        cast_shapes.append(jax.ShapeDtypeStruct(w.shape[1:], BF16))
    proj3 = proj.reshape(batch, seq, proj.shape[-1])
    gcol3 = gcol.reshape(batch, seq, LANES)
    nch = seq // DN_CHUNK
    dk = B_HEAD_DIM
    tok = lambda blk: pl.BlockSpec((None, tb, LANES), lambda b, h, t: (b, t, blk + h))
    halo = lambda blk: pl.BlockSpec(
        (None, SUBLANES, LANES),
        lambda b, h, t: (b, jnp.maximum(t * (tb // SUBLANES) - 1, 0), blk + h))
    cw = lambda blk: pl.BlockSpec((CONV_K, LANES), lambda b, h, t: (0, blk + h))
    per_chunk = lambda rows: pl.BlockSpec((None, None, rows, LANES), lambda b, h, t: (b, h, t, 0))
    kw, nn, qe, ol, gl, *casts = pl.pallas_call(
        functools.partial(_gdn_pre_kernel, n_cast=len(cast_list), n_cast_step=n_cast_step),
        out_shape=(jax.ShapeDtypeStruct((batch, B_HEADS, nch * dk, dk), BF16),
                   jax.ShapeDtypeStruct((batch, B_HEADS, nch * dk, dk), F32),
                   jax.ShapeDtypeStruct((batch, B_HEADS, seq, dk), BF16),
                   jax.ShapeDtypeStruct((batch, B_HEADS, seq, dk), F32),
                   jax.ShapeDtypeStruct((batch, B_HEADS, nch, LANES), F32),
                   *cast_shapes),
        grid=(batch, B_HEADS, nt),
        in_specs=[tok(BLK_QB), tok(BLK_KB), tok(BLK_VB),
                  halo(BLK_QB), halo(BLK_KB), halo(BLK_VB),
                  pl.BlockSpec((None, tb, LANES), lambda b, h, t: (b, t, 0)),
                  pl.BlockSpec((GATE_ROWS, tb), lambda b, h, t: (0, b * nt + t)),
                  cw(0), cw(B_HEADS), cw(2 * B_HEADS), *cast_in_specs],
        out_specs=(per_chunk(GDN_NC * dk), per_chunk(GDN_NC * dk), per_chunk(tb), per_chunk(tb),
                   per_chunk(GDN_NC), *cast_out_specs),
        scratch_shapes=[pltpu.VMEM((3, SUBLANES + tb, LANES), F32)],
        compiler_params=_params(("parallel", "parallel", "arbitrary"), 40),
        name="gdn_pre",
    )(proj3, proj3, proj3, proj3, proj3, proj3, gcol3, grow, conv_w, conv_w, conv_w,
      *[w for w, _ in cast_list])

    heads = lambda rows: pl.BlockSpec((None, B_HEADS, rows, LANES), lambda b, t: (b, 0, t, 0))
    ts = SCAN_TB
    nz = B_W // Z_SPLIT
    assert (BLK_Z * LANES) % Z_SPLIT == 0
    z_blk0 = BLK_Z * LANES // Z_SPLIT
    out = pl.pallas_call(
        _gdn_scan_kernel,
        out_shape=jax.ShapeDtypeStruct((batch, seq, B_W), BF16),
        grid=(batch, seq // ts),
        in_specs=[heads(SCAN_NC * dk), heads(SCAN_NC * dk), heads(ts), heads(ts), heads(SCAN_NC)]
        + [pl.BlockSpec((None, ts, Z_SPLIT), lambda b, t, k=k: (b, t, z_blk0 + k)) for k in range(nz)]
        + [pl.BlockSpec((1, dk), lambda b, t: (0, 0))],
        out_specs=pl.BlockSpec((None, ts, B_W), lambda b, t: (b, t, 0)),
        scratch_shapes=[pltpu.VMEM((B_HEADS, dk, dk), F32)],
        compiler_params=_params(("parallel", "arbitrary"), 36),
        name="gdn_scan",
    )(kw, nn, qe, ol, gl, *([proj3] * nz), onorm)
    return out.reshape(batch * seq, B_W), [casts[order.index(idx)] for idx in range(len(order))]


ODD_TM = 512
ODD_TN = 1024
ODD_NV = D_MODEL // ODD_TN


def _odd_mixer_kernel(x_ref, g_ref, win_ref, lg_ref, lb_ref, ws_ref, bs_ref, wout_ref,
                      o_ref, hn_ref, v_ref, y_ref, gated_ref):
    j = pl.program_id(1)
    cc = C_CHUNK
    gpb = ODD_TN // C_GROUP_DIM

    @pl.when(j == 0)
    def _():
        hn_ref[...] = _rms_rows(x_ref[...], g_ref[...]).astype(BF16)

    @pl.when(j < ODD_NV)
    def _():
        v_ref[j] = _gelu(_dot(hn_ref[...], win_ref[...]))

    for blk in range(ODD_NV):
        @pl.when(j == ODD_NV + blk)
        def _(blk=blk):
            if blk == 0:
                v = jnp.concatenate([v_ref[k] for k in range(ODD_NV)], axis=-1)
                mu = jnp.mean(v, axis=-1, keepdims=True)
                vc = v - mu
                var = jnp.mean(vc * vc, axis=-1, keepdims=True)
                y_ref[...] = (vc * lax.rsqrt(var + EPS) * lg_ref[...] + lb_ref[...]).astype(BF16)
            u = _gelu(_dot(hn_ref[...], win_ref[...]))
            row = lax.broadcasted_iota(jnp.int32, (cc, cc), 0)
            col = lax.broadcasted_iota(jnp.int32, (cc, cc), 1)
            bs = bs_ref[...]
            for gl in range(gpb):
                gi = blk * gpb + gl
                ws = jnp.where(row >= col, ws_ref[gi], 0.0).astype(BF16)
                bias = bs[:, gi:gi + 1]
                cs = slice(gi * C_GROUP_DIM, (gi + 1) * C_GROUP_DIM)
                us = slice(gl * C_GROUP_DIM, (gl + 1) * C_GROUP_DIM)
                for n in range(ODD_TM // cc):
                    rs = slice(n * cc, (n + 1) * cc)
                    mixed = _dot(ws, y_ref[rs, cs]) + bias
                    gated_ref[rs, cs] = (u[rs, us] * mixed).astype(BF16)

    @pl.when(j == 2 * ODD_NV)
    def _():
        o_ref[...] = x_ref[...] + _dot(gated_ref[...], wout_ref[...])


def _odd_mixer(x, g, w_in3, ln_g, ln_b, w_s, b_s_t, w_out3):
    n, d = x.shape
    tm, tn, nv = ODD_TM, ODD_TN, ODD_NV
    win_blk = lambda i, j: (0, 0, jnp.where(j < nv, j + nv, jnp.minimum(j - nv, nv - 1)))
    const = lambda *shape: pl.BlockSpec(shape, lambda i, j: (0,) * len(shape))
    return pl.pallas_call(
        _odd_mixer_kernel,
        out_shape=jax.ShapeDtypeStruct((n, d), F32),
        grid=(n // tm, 2 * nv + 1),
        in_specs=[pl.BlockSpec((tm, d), lambda i, j: (i, 0)),
                  const(1, d),
                  pl.BlockSpec((None, d, tn), win_blk),
                  const(1, d), const(1, d),
                  const(C_GROUPS, C_CHUNK, C_CHUNK),
                  const(C_CHUNK, C_GROUPS),
                  pl.BlockSpec((None, d, d), lambda i, j: (0, 0, 0))],
        out_specs=pl.BlockSpec((tm, d), lambda i, j: (i, 0)),
        scratch_shapes=[pltpu.VMEM((tm, d), BF16),
                        pltpu.VMEM((nv, tm, tn), F32),
                        pltpu.VMEM((tm, d), BF16),
                        pltpu.VMEM((tm, d), BF16)],
        compiler_params=_params(("parallel", "arbitrary"), 56),
        name="odd_mixer",
    )(x, g, w_in3, ln_g, ln_b, w_s, b_s_t, w_out3)


def kernel(x, even_norm, even_w_in, even_conv, even_a_log, even_dt_bias, even_sinks,
           even_onorm, even_w_out, odd_norm, odd_w_in, odd_ln_g, odd_ln_b, odd_w_s,
           odd_b_s, odd_w_out, ffn_norm, ffn_w_gate, ffn_w_up, ffn_w_down, final_norm):
    batch, seq, d = x.shape
    n = batch * seq
    h = x.reshape(n, d)
    row = lambda a: a.reshape(1, -1).astype(F32)

    w_bg = even_w_in[0, :, EVEN_MAIN:]
    w_gate = jnp.pad(jnp.concatenate([w_bg, w_bg[:, B_HEADS:]], axis=1),
                     ((0, 0), (0, LANES - GATE_ROWS))).astype(BF16)
    proj, gates = _norm_proj_even(h, row(even_norm[0]), even_w_in.astype(BF16), w_gate)
    out_a = _swa(proj, even_sinks[0].astype(F32), batch, seq)

    def gate_row(a):
        a = a.astype(F32)
        return jnp.pad(jnp.concatenate([a, a]), (GATE_GAM, LANES - GATE_ROWS)).reshape(1, LANES)

    out_b, (w_eo, wg0, wu0, wd0, w_oi, w_oo, wg1, wu1, wd1) = _gdn(
        proj, gates, even_conv[0], gate_row(even_a_log[0]), gate_row(even_dt_bias[0]),
        row(even_onorm[0]), batch, seq,
        [(even_w_out, 0), (ffn_w_gate, 0), (ffn_w_up, 0), (ffn_w_down, 0), (odd_w_in, 0), (odd_w_out, 0),
         (ffn_w_gate, 1), (ffn_w_up, 1), (ffn_w_down, 1)])
    h = _proj_residual([out_a, out_b], w_eo[None], h)
    h = _ffn(h, row(ffn_norm[0]), wg0, wu0, wd0, row(final_norm), final_norm=False)

    h = _odd_mixer(h, row(odd_norm[0]), w_oi[None], row(odd_ln_g[0]), row(odd_ln_b[0]),
                   odd_w_s[0], odd_b_s[0].T, w_oo[None])
    h = _ffn(h, row(ffn_norm[1]), wg1, wu1, wd1, row(final_norm), final_norm=True)
    return h.reshape(batch, seq, d)
```

```python
import functools

import jax
import jax.numpy as jnp
from jax import lax
from jax.experimental import pallas as pl
from jax.experimental.pallas import tpu as pltpu

F32 = jnp.float32
BF16 = jnp.bfloat16

D_MODEL = 2048
A_HEADS = 16
A_KV_HEADS = 2
A_HEAD_DIM = 64
WINDOW = 128
B_HEADS = 8
B_HEAD_DIM = 128
CONV_K = 4
DN_CHUNK = 64
DN_CHUNK_LOG2 = 6
C_GROUPS = 8
C_CHUNK = 128
C_GROUP_DIM = D_MODEL // C_GROUPS
EPS = 1e-6

LANES = 128
SUBLANES = 8
A_Q = A_HEADS * A_HEAD_DIM
A_KV = A_KV_HEADS * A_HEAD_DIM
B_W = B_HEADS * B_HEAD_DIM
EVEN_MAIN = A_Q + 2 * A_KV + 4 * B_W
BLK_KA = A_Q // LANES
BLK_VA = BLK_KA + A_KV // LANES
BLK_QB = BLK_VA + A_KV // LANES
BLK_KB = BLK_QB + B_HEADS
BLK_VB = BLK_KB + B_HEADS
BLK_Z = BLK_VB + B_HEADS
Z_SPLIT = 2 * LANES
GATE_BETA = 0
GATE_GAM = B_HEADS
GATE_LAST = 2 * B_HEADS
GATE_ROWS = 3 * B_HEADS

MIB = 1024 * 1024


def _params(sem, vmem_mib):
    return pltpu.CompilerParams(dimension_semantics=sem, vmem_limit_bytes=vmem_mib * MIB)


def _cast_specs(w, layer, nr, nc, tile_of):
    rows, cols = w.shape[1] // nr, w.shape[2] // nc
    assert w.shape[1] % nr == 0 and w.shape[2] % nc == 0
    assert rows % (2 * SUBLANES) == 0 and cols % LANES == 0
    return (pl.BlockSpec((None, rows, cols), lambda *g: (layer, *tile_of(*g))),
            pl.BlockSpec((rows, cols), lambda *g: tuple(tile_of(*g))),
            jax.ShapeDtypeStruct(w.shape[1:], BF16))


def _cast_tiles(srcs, dsts):
    for src, dst in zip(srcs, dsts):
        dst[...] = src[...].astype(dst.dtype)


def _dot(a, b):
    return jnp.dot(a, b, preferred_element_type=F32)


def _dot_nt(a, b):
    return lax.dot_general(a, b, (((1,), (1,)), ((), ())), preferred_element_type=F32)


def _rms_rows(x, g):
    ms = jnp.mean(x * x, axis=-1, keepdims=True)
    return x * lax.rsqrt(ms + EPS) * g


def _silu(x):
    return x * jax.nn.sigmoid(x)


def _gelu(x):
    return x * (lax.erf(x * (2.0 ** -0.5)) + 1.0) * 0.5


def _norm_proj_kernel(x_ref, g_ref, w_ref, wg_ref, o_ref, og_ref, hn_ref):
    @pl.when(pl.program_id(1) == 0)
    def _():
        hn = _rms_rows(x_ref[...], g_ref[...]).astype(BF16)
        hn_ref[...] = hn
        og_ref[...] = _dot(hn, wg_ref[...])

    o_ref[...] = _dot(hn_ref[...], w_ref[...])


def _norm_proj_even(x, g, w_in, w_gate, *, tm=1024, tn=1792):
    n, d = x.shape
    nout = EVEN_MAIN
    assert nout % tn == 0
    return pl.pallas_call(
        _norm_proj_kernel,
        out_shape=(jax.ShapeDtypeStruct((n, nout), F32),
                   jax.ShapeDtypeStruct((n, LANES), F32)),
        grid=(n // tm, nout // tn),
        in_specs=[pl.BlockSpec((tm, d), lambda i, j: (i, 0)),
                  pl.BlockSpec((1, d), lambda i, j: (0, 0)),
                  pl.BlockSpec((d, tn), lambda i, j: (0, j)),
                  pl.BlockSpec((d, LANES), lambda i, j: (0, 0))],
        out_specs=(pl.BlockSpec((tm, tn), lambda i, j: (i, j)),
                   pl.BlockSpec((tm, LANES), lambda i, j: (i, 0))),
        scratch_shapes=[pltpu.VMEM((tm, d), BF16)],
        compiler_params=_params(("parallel", "arbitrary"), 56),
        name="norm_proj_even",
    )(x, g, w_in, w_gate)


def _proj_res_kernel(*refs, n_in, n_cast):
    a_refs = refs[:n_in]
    w_refs = refs[n_in:2 * n_in]
    res_ref = refs[2 * n_in]
    cast_in = refs[2 * n_in + 1:2 * n_in + 1 + n_cast]
    o_ref = refs[2 * n_in + 1 + n_cast]
    cast_out = refs[2 * n_in + 2 + n_cast:]
    _cast_tiles(cast_in, cast_out)
    acc = res_ref[...]
    for a_ref, w_ref in zip(a_refs, w_refs):
        acc = acc + _dot(a_ref[...], w_ref[...])
    o_ref[...] = acc


def _proj_residual(a_list, w3, res, cast_list, *, tm=512):
    n, d = res.shape
    n_in = len(a_list)
    kk = a_list[0].shape[1]
    assert all(a.shape[1] == kk for a in a_list) and w3.shape[1] == n_in * kk
    ni = n // tm
    cast = [_cast_specs(w, layer, ni, 1, lambda i: (i, 0)) for w, layer in cast_list]
    in_specs = ([pl.BlockSpec((tm, kk), lambda i: (i, 0)) for _ in a_list]
                + [pl.BlockSpec((None, kk, d), lambda i, k=k: (0, k, 0)) for k in range(n_in)]
                + [pl.BlockSpec((tm, d), lambda i: (i, 0))] + [c[0] for c in cast])
    res_all = pl.pallas_call(
        functools.partial(_proj_res_kernel, n_in=n_in, n_cast=len(cast)),
        out_shape=(jax.ShapeDtypeStruct((n, d), F32), *[c[2] for c in cast]),
        grid=(ni,),
        in_specs=in_specs,
        out_specs=(pl.BlockSpec((tm, d), lambda i: (i, 0)), *[c[1] for c in cast]),
        compiler_params=_params(("parallel",), 40),
        name="proj_residual",
    )(*a_list, *([w3] * n_in), res, *[w for w, _ in cast_list])
    return res_all[0], list(res_all[1:])


def _ffn_kernel(x_ref, g_ref, wg_ref, wu_ref, wd_ref, fg_ref, *rest, final_norm, n_cast):
    cast_in = rest[:n_cast]
    o_ref = rest[n_cast]
    cast_out = rest[n_cast + 1:2 * n_cast + 1]
    hn_ref = rest[2 * n_cast + 1]
    f = pl.program_id(1)
    _cast_tiles(cast_in, cast_out)

    @pl.when(f == 0)
    def _():
        x = x_ref[...]
        hn_ref[...] = _rms_rows(x, g_ref[...]).astype(BF16)
        o_ref[...] = x

    hn = hn_ref[...]
    a = (_silu(_dot(hn, wg_ref[...])) * _dot(hn, wu_ref[...])).astype(BF16)
    o_ref[...] += _dot(a, wd_ref[...])

    if final_norm:
        @pl.when(f == pl.num_programs(1) - 1)
        def _():
            o_ref[...] = _rms_rows(o_ref[...], fg_ref[...])


def _ffn(x, g, w_gate, w_up, w_down, final_g, cast_list, *, final_norm, tm=1024, tf=512):
    n, d = x.shape
    dff = w_gate.shape[1]
    ni, nf = n // tm, dff // tf
    cast = [_cast_specs(w, layer, ni, nf, lambda i, f: (i, f)) if w.shape[1] == d else
            _cast_specs(w, layer, nf, ni, lambda i, f: (f, i)) for w, layer in cast_list]
    res = pl.pallas_call(
        functools.partial(_ffn_kernel, final_norm=final_norm, n_cast=len(cast)),
        out_shape=(jax.ShapeDtypeStruct((n, d), F32), *[c[2] for c in cast]),
        grid=(ni, nf),
        in_specs=[pl.BlockSpec((tm, d), lambda i, f: (i, 0)),
                  pl.BlockSpec((1, d), lambda i, f: (0, 0)),
                  pl.BlockSpec((d, tf), lambda i, f: (0, f)),
                  pl.BlockSpec((d, tf), lambda i, f: (0, f)),
                  pl.BlockSpec((tf, d), lambda i, f: (f, 0)),
                  pl.BlockSpec((1, d), lambda i, f: (0, 0)), *[c[0] for c in cast]],
        out_specs=(pl.BlockSpec((tm, d), lambda i, f: (i, 0)), *[c[1] for c in cast]),
        scratch_shapes=[pltpu.VMEM((tm, d), BF16)],
        compiler_params=_params(("parallel", "arbitrary"), 58 if cast else 56),
        name="swiglu_ffn",
    )(x, g, w_gate, w_up, w_down, final_g, *[w for w, _ in cast_list])
    return res[0], list(res[1:])


SWA_SUB = 2

def _swa_kernel(sink_ref, q_ref, k_ref, v_ref, kp_ref, vp_ref, o_ref):
    first = pl.program_id(1) == 0
    w = WINDOW
    grp = A_HEADS // A_KV_HEADS
    lane = lax.broadcasted_iota(jnp.int32, (w, LANES), 1)
    low = lane < A_HEAD_DIM
    row = lax.broadcasted_iota(jnp.int32, (w, w), 0)
    col = lax.broadcasted_iota(jnp.int32, (w, w), 1)
    cur_ok = jnp.concatenate([col <= row] * 2, axis=0)

    def dup_half(x, head):
        swapped = pltpu.roll(x, A_HEAD_DIM, 1)
        if head == 0:
            return jnp.where(low, x, swapped)
        return jnp.where(low, swapped, x)

    kvs = [(kp_ref[...], vp_ref[...])] + [(k_ref[sb * w:(sb + 1) * w, :], v_ref[sb * w:(sb + 1) * w, :])
                                           for sb in range(SWA_SUB)]
    dups = [[(dup_half(kk, kvh), dup_half(vv, kvh)) for kk, vv in kvs] for kvh in range(A_KV_HEADS)]
    k2s = [[jnp.concatenate([d[sb + 1][0], d[sb][0]], axis=0).astype(BF16) for sb in range(SWA_SUB)]
           for d in dups]
    v2s = [[jnp.concatenate([d[sb + 1][1], d[sb][1]], axis=0).astype(BF16) for sb in range(SWA_SUB)]
           for d in dups]
    scale = A_HEAD_DIM ** -0.5
    zero = jnp.zeros((), BF16)
    chains = [(sb, pr) for sb in range(SWA_SUB) for pr in range(A_HEADS // 2)]
    s2s, sinks = [], []
    for sb, pr in chains:
        pair = q_ref[sb * w:(sb + 1) * w, pr * LANES:(pr + 1) * LANES] * scale
        qst = jnp.concatenate([jnp.where(low, pair, 0.0), jnp.where(low, 0.0, pair)], axis=0)
        s2s.append(_dot_nt(qst.astype(BF16), k2s[2 * pr // grp][sb]))
        sinks.append(jnp.concatenate([jnp.full((w, w), sink_ref[2 * pr], F32),
                                      jnp.full((w, w), sink_ref[2 * pr + 1], F32)], axis=0))
    p2s = []
    for (sb, pr), s2, sink in zip(chains, s2s, sinks):
        prev = jnp.where(first, -jnp.inf, s2[:, w:]) if sb == 0 else s2[:, w:]
        s = jnp.where(cur_ok, s2[:, :w], prev)
        m = jnp.maximum(jnp.max(s, axis=-1, keepdims=True), sink)
        p = jnp.exp(s - m)
        denom = jnp.sum(p, axis=-1, keepdims=True) + jnp.exp(sink - m)
        pb = (p * (1.0 / denom)).astype(BF16)
        p2s.append(jnp.concatenate([jnp.where(cur_ok, pb, zero), jnp.where(cur_ok, zero, pb)], axis=-1))
    for (sb, pr), p2 in zip(chains, p2s):
        o = _dot(p2, v2s[2 * pr // grp][sb])
        o_ref[sb * w:(sb + 1) * w, pr * LANES:(pr + 1) * LANES] = (
            jnp.where(low, o[:w], o[w:]).astype(o_ref.dtype))


def _swa(proj, sinks, batch, seq):
    tq = SWA_SUB * WINDOW
    nb = seq // tq
    proj3 = proj.reshape(batch, seq, proj.shape[-1])
    spec = lambda blk, prev: (
        pl.BlockSpec((None, WINDOW, LANES), lambda b, n: (b, jnp.maximum(n * SWA_SUB - 1, 0), blk))
        if prev else pl.BlockSpec((None, tq, LANES), lambda b, n: (b, n, blk)))
    out = pl.pallas_call(
        _swa_kernel,
        out_shape=jax.ShapeDtypeStruct((batch, seq, A_Q), BF16),
        grid=(batch, nb),
        in_specs=[pl.BlockSpec(memory_space=pltpu.SMEM),
                  pl.BlockSpec((None, tq, A_Q), lambda b, n: (b, n, 0)),
                  spec(BLK_KA, False), spec(BLK_VA, False),
                  spec(BLK_KA, True), spec(BLK_VA, True)],
        out_specs=pl.BlockSpec((None, tq, A_Q), lambda b, n: (b, n, 0)),
        compiler_params=_params(("parallel", "arbitrary"), 16),
        name="swa_sink_attention",
    )(sinks, proj3, proj3, proj3, proj3, proj3)
    return out.reshape(batch * seq, A_Q)


GATE_TB = 512
GDN_TB = 1024
GDN_NC = GDN_TB // DN_CHUNK
GDN_GRP = 128
GRP_NC = GDN_GRP // DN_CHUNK
SCAN_TB = 512
SCAN_NC = SCAN_TB // DN_CHUNK


def _split3(x):
    hi = x.astype(BF16)
    r1 = x - hi.astype(F32)
    mid = r1.astype(BF16)
    lo = (r1 - mid.astype(F32)).astype(BF16)
    return hi, mid, lo


def _gate_prep_kernel(gate_ref, alog_ref, dt_ref, o_ref, ot_ref):
    tb = GATE_TB
    gate = gate_ref[...]
    lane = lax.broadcasted_iota(jnp.int32, (tb, LANES), 1)
    beta = jax.nn.sigmoid(gate)
    g = -jnp.exp(alog_ref[...]) * jax.nn.softplus(gate + dt_ref[...])
    row = lax.broadcasted_iota(jnp.int32, (tb, tb), 0)
    col = lax.broadcasted_iota(jnp.int32, (tb, tb), 1)
    same = (row >> DN_CHUNK_LOG2) == (col >> DN_CHUNK_LOG2)
    tri = jnp.where(jnp.logical_and(same, row >= col), 1.0, 0.0).astype(BF16)
    ones = jnp.where(same, 1.0, 0.0).astype(BF16)
    r = _dot(jnp.concatenate([tri, ones], axis=0), jnp.concatenate(_split3(g), axis=-1))
    s = r[:, :LANES] + r[:, LANES:2 * LANES] + r[:, 2 * LANES:]
    out = jnp.where(lane < GATE_GAM, beta, jnp.where(lane < GATE_LAST, s[:tb], s[tb:]))
    o_ref[...] = out
    ot_ref[...] = out.T[:GATE_ROWS, :]


def _gate_prep(gates, alog_row, dt_row):
    n = gates.shape[0]
    tb = GATE_TB
    row = pl.BlockSpec((1, LANES), lambda i: (0, 0))
    return pl.pallas_call(
        _gate_prep_kernel,
        out_shape=(jax.ShapeDtypeStruct((n, LANES), F32),
                   jax.ShapeDtypeStruct((GATE_ROWS, n), F32)),
        grid=(n // tb,),
        in_specs=[pl.BlockSpec((tb, LANES), lambda i: (i, 0)), row, row],
        out_specs=(pl.BlockSpec((tb, LANES), lambda i: (i, 0)),
                   pl.BlockSpec((GATE_ROWS, tb), lambda i: (0, i))),
        compiler_params=_params(("parallel",), 8),
        name="gdn_gate_prep",
    )(gates, alog_row, dt_row)


def _conv_silu(x_ref, tail, w, buf_ref):
    t = x_ref.shape[0]
    x = x_ref[...]
    buf_ref[0:SUBLANES, :] = tail
    buf_ref[SUBLANES:, :] = x
    acc = x * w[CONV_K - 1:CONV_K, :]
    for j in range(CONV_K - 1):
        acc = acc + buf_ref[pl.ds(SUBLANES - (CONV_K - 1 - j), t), :] * w[j:j + 1, :]
    return _silu(acc)


def _lane_col(x, idx):
    lane = lax.broadcasted_iota(jnp.int32, x.shape, 1)
    return jnp.sum(jnp.where(lane == idx, x, 0.0), axis=-1, keepdims=True)


def _gdn_pre_kernel(q_ref, k_ref, v_ref, hq_ref, hk_ref, hv_ref, gc_ref, gr_ref,
                    cq_ref, ck_ref, cv_ref, *rest, n_cast, n_cast_step):
    cast_in = rest[:n_cast]
    kw_ref, nn_ref, qe_ref, ol_ref, gl_ref = rest[n_cast:n_cast + 5]
    cast_out = rest[n_cast + 5:2 * n_cast + 5]
    buf_ref = rest[2 * n_cast + 5]
    h = pl.program_id(1)
    first = pl.program_id(2) == 0
    gs = GDN_GRP

    _cast_tiles(cast_in[:n_cast_step], cast_out[:n_cast_step])

    @pl.when(first)
    def _():
        _cast_tiles(cast_in[n_cast_step:], cast_out[n_cast_step:])

    def conv(x_ref, halo_ref, w_ref, slot):
        tail = jnp.where(first, 0.0, halo_ref[...])
        return _conv_silu(x_ref, tail, w_ref[...], buf_ref.at[slot])

    q = conv(q_ref, hq_ref, cq_ref, 0)
    k = conv(k_ref, hk_ref, ck_ref, 1)
    v = conv(v_ref, hv_ref, cv_ref, 2)
    q = q * lax.rsqrt(jnp.sum(q * q, axis=-1, keepdims=True) + EPS) * (B_HEAD_DIM ** -0.5)
    k = k * lax.rsqrt(jnp.sum(k * k, axis=-1, keepdims=True) + EPS)

    gall = gc_ref[...]
    beta = _lane_col(gall, h + GATE_BETA)
    gam = _lane_col(gall, h + GATE_GAM)
    gam_end = _lane_col(gall, h + GATE_LAST)
    gam_row = gr_ref[pl.ds(h + GATE_GAM, 1), :]
    egam = jnp.exp(gam)
    q_dec = q * egam
    k_dec = k * jnp.exp(gam_end - gam)
    rhs_all = jnp.concatenate([v * beta, k * (beta * egam)], axis=-1)
    ends = gc_ref[pl.ds(DN_CHUNK - 1, GDN_NC, stride=DN_CHUNK), :]
    gl_ref[...] = jnp.broadcast_to(jnp.exp(_lane_col(ends, h + GATE_GAM)), (GDN_NC, LANES))

    row = lax.broadcasted_iota(jnp.int32, (gs, gs), 0)
    col = lax.broadcasted_iota(jnp.int32, (gs, gs), 1)
    same = (row >> DN_CHUNK_LOG2) == (col >> DN_CHUNK_LOG2)
    incl = jnp.logical_and(same, row >= col)
    strict = jnp.logical_and(same, row > col)
    trow = lax.broadcasted_iota(jnp.int32, (GRP_NC * B_HEAD_DIM, gs), 0)
    tcol = lax.broadcasted_iota(jnp.int32, (GRP_NC * B_HEAD_DIM, gs), 1)
    own = (trow >> 7) == (tcol >> DN_CHUNK_LOG2)

    groups = range(GDN_TB // gs)
    sls = [slice(grp * gs, (grp + 1) * gs) for grp in groups]
    kbs = [k[sl].astype(BF16) for sl in sls]
    scs = [_dot_nt(jnp.concatenate([q[sl].astype(BF16), kb], axis=0), kb)
           for sl, kb in zip(sls, kbs)]
    decays = [jnp.exp(jnp.where(incl, gam[sl] - gam_row[:, sl], -jnp.inf)) for sl in sls]
    qks = [(sc[:gs] * dec).astype(BF16) for sc, dec in zip(scs, decays)]
    xs = [jnp.where(strict, (-beta[sl]) * sc[gs:] * dec, 0.0)
          for sl, sc, dec in zip(sls, scs, decays)]
    ys = [rhs_all[sl] for sl in sls]
    for _ in range(DN_CHUNK_LOG2 - 1):
        xbs = [x.astype(BF16) for x in xs]
        rs = [_dot(xb, jnp.concatenate([xb, y.astype(BF16)], axis=-1))
              for xb, y in zip(xbs, ys)]
        xs = [r[:, :gs] for r in rs]
        ys = [y + r[:, gs:] for y, r in zip(ys, rs)]
    ys = [y + _dot(x.astype(BF16), y.astype(BF16)) for x, y in zip(xs, ys)]
    ybs = [y.astype(BF16) for y in ys]
    es = [_dot(qk, yb) for qk, yb in zip(qks, ybs)]
    kdt4s = [jnp.where(own, jnp.concatenate([k_dec[sl].T] * GRP_NC, axis=0), 0.0).astype(BF16)
             for sl in sls]
    fs = [_dot(kdt4, yb) for kdt4, yb in zip(kdt4s, ybs)]
    for grp, sl, e, f in zip(groups, sls, es, fs):
        ol_ref[sl, :] = e[:, :B_HEAD_DIM]
        qe_ref[sl, :] = (q_dec[sl] - e[:, B_HEAD_DIM:]).astype(qe_ref.dtype)
        rws = slice(grp * GRP_NC * B_HEAD_DIM, (grp + 1) * GRP_NC * B_HEAD_DIM)
        nn_ref[rws, :] = f[:, :B_HEAD_DIM]
        kw_ref[rws, :] = f[:, B_HEAD_DIM:].astype(kw_ref.dtype)


def _gdn_scan_kernel(kw_ref, nn_ref, qe_ref, ol_ref, gl_ref, *rest):
    nz = B_W // Z_SPLIT
    z_refs = rest[:nz]
    onorm_ref, o_ref, state_ref = rest[nz:]
    c = DN_CHUNK
    dk = B_HEAD_DIM
    hpz = Z_SPLIT // dk

    @pl.when(pl.program_id(1) == 0)
    def _():
        state_ref[...] = jnp.zeros_like(state_ref)

    onorm = onorm_ref[...]
    for n in range(SCAN_NC):
        ts = slice(n * c, (n + 1) * c)
        ds = slice(n * dk, (n + 1) * dk)
        for h in range(B_HEADS):
            hs = slice(h * dk, (h + 1) * dk)
            s = state_ref[h]
            r = _dot(jnp.concatenate([kw_ref[h, ds, :], qe_ref[h, ts, :]], axis=0), s.astype(BF16))
            state_ref[h] = s * gl_ref[h, n:n + 1, :] + nn_ref[h, ds, :] - r[:dk]
            o = ol_ref[h, ts, :] + r[dk:]
            o = o * lax.rsqrt(jnp.mean(o * o, axis=-1, keepdims=True) + EPS) * onorm
            z = z_refs[h // hpz][ts, (h % hpz) * dk:(h % hpz + 1) * dk]
            o_ref[ts, hs] = (o * _silu(z)).astype(o_ref.dtype)


def _gdn(proj, gates, conv_w, alog_row, dt_row, onorm, batch, seq, cast_list):
    gcol, grow = _gate_prep(gates, alog_row, dt_row)
    tb = GDN_TB
    nt = seq // tb
    bf16_rows = 2 * SUBLANES
    n_steps = batch * B_HEADS * nt
    per_step = lambda wl: wl[0].shape[1] % (n_steps * bf16_rows) == 0
    order = sorted(range(len(cast_list)), key=lambda idx: not per_step(cast_list[idx]))
    cast_list = [cast_list[idx] for idx in order]
    n_cast_step = sum(per_step(wl) for wl in cast_list)
    cast = [_cast_specs(w, layer, n_steps, 1, lambda b, h, t: ((b * B_HEADS + h) * nt + t, 0))
            if per_step((w, layer)) else
            _cast_specs(w, layer, batch * B_HEADS, 1, lambda b, h, t: (b * B_HEADS + h, 0))
            for w, layer in cast_list]
    cast_in_specs, cast_out_specs, cast_shapes = ([c[k] for c in cast] for k in range(3))
    proj3 = proj.reshape(batch, seq, proj.shape[-1])
    gcol3 = gcol.reshape(batch, seq, LANES)
    nch = seq // DN_CHUNK
    dk = B_HEAD_DIM
    tok = lambda blk: pl.BlockSpec((None, tb, LANES), lambda b, h, t: (b, t, blk + h))
    halo = lambda blk: pl.BlockSpec(
        (None, SUBLANES, LANES),
        lambda b, h, t: (b, jnp.maximum(t * (tb // SUBLANES) - 1, 0), blk + h))
    cw = lambda blk: pl.BlockSpec((CONV_K, LANES), lambda b, h, t: (0, blk + h))
    per_chunk = lambda rows: pl.BlockSpec((None, None, rows, LANES), lambda b, h, t: (b, h, t, 0))
    kw, nn, qe, ol, gl, *casts = pl.pallas_call(
        functools.partial(_gdn_pre_kernel, n_cast=len(cast_list), n_cast_step=n_cast_step),
        out_shape=(jax.ShapeDtypeStruct((batch, B_HEADS, nch * dk, dk), BF16),
                   jax.ShapeDtypeStruct((batch, B_HEADS, nch * dk, dk), F32),
                   jax.ShapeDtypeStruct((batch, B_HEADS, seq, dk), BF16),
                   jax.ShapeDtypeStruct((batch, B_HEADS, seq, dk), F32),
                   jax.ShapeDtypeStruct((batch, B_HEADS, nch, LANES), F32),
                   *cast_shapes),
        grid=(batch, B_HEADS, nt),
        in_specs=[tok(BLK_QB), tok(BLK_KB), tok(BLK_VB),
                  halo(BLK_QB), halo(BLK_KB), halo(BLK_VB),
                  pl.BlockSpec((None, tb, LANES), lambda b, h, t: (b, t, 0)),
                  pl.BlockSpec((GATE_ROWS, tb), lambda b, h, t: (0, b * nt + t)),
                  cw(0), cw(B_HEADS), cw(2 * B_HEADS), *cast_in_specs],
        out_specs=(per_chunk(GDN_NC * dk), per_chunk(GDN_NC * dk), per_chunk(tb), per_chunk(tb),
                   per_chunk(GDN_NC), *cast_out_specs),
        scratch_shapes=[pltpu.VMEM((3, SUBLANES + tb, LANES), F32)],
        compiler_params=_params(("parallel", "parallel", "arbitrary"), 40),
        name="gdn_pre",
    )(proj3, proj3, proj3, proj3, proj3, proj3, gcol3, grow, conv_w, conv_w, conv_w,
      *[w for w, _ in cast_list])

    heads = lambda rows: pl.BlockSpec((None, B_HEADS, rows, LANES), lambda b, t: (b, 0, t, 0))
    ts = SCAN_TB
    nz = B_W // Z_SPLIT
    assert (BLK_Z * LANES) % Z_SPLIT == 0
    z_blk0 = BLK_Z * LANES // Z_SPLIT
    out = pl.pallas_call(
        _gdn_scan_kernel,
        out_shape=jax.ShapeDtypeStruct((batch, seq, B_W), BF16),
        grid=(batch, seq // ts),
        in_specs=[heads(SCAN_NC * dk), heads(SCAN_NC * dk), heads(ts), heads(ts), heads(SCAN_NC)]
        + [pl.BlockSpec((None, ts, Z_SPLIT), lambda b, t, k=k: (b, t, z_blk0 + k)) for k in range(nz)]
        + [pl.BlockSpec((1, dk), lambda b, t: (0, 0))],
        out_specs=pl.BlockSpec((None, ts, B_W), lambda b, t: (b, t, 0)),
        scratch_shapes=[pltpu.VMEM((B_HEADS, dk, dk), F32)],
        compiler_params=_params(("parallel", "arbitrary"), 36),
        name="gdn_scan",
    )(kw, nn, qe, ol, gl, *([proj3] * nz), onorm)
    return out.reshape(batch * seq, B_W), [casts[order.index(idx)] for idx in range(len(order))]


ODD_TM = 512
ODD_TN = 1024
ODD_NV = D_MODEL // ODD_TN


def _odd_mixer_kernel(x_ref, g_ref, win_ref, lg_ref, lb_ref, ws_ref, bs_ref, wout_ref,
                      o_ref, hn_ref, v_ref, y_ref, gated_ref):
    j = pl.program_id(1)
    cc = C_CHUNK
    gpb = ODD_TN // C_GROUP_DIM

    @pl.when(j == 0)
    def _():
        hn_ref[...] = _rms_rows(x_ref[...], g_ref[...]).astype(BF16)

    @pl.when(j < ODD_NV)
    def _():
        v_ref[j] = _gelu(_dot(hn_ref[...], win_ref[...]))

    for blk in range(ODD_NV):
        @pl.when(j == ODD_NV + blk)
        def _(blk=blk):
            if blk == 0:
                v = jnp.concatenate([v_ref[k] for k in range(ODD_NV)], axis=-1)
                mu = jnp.mean(v, axis=-1, keepdims=True)
                vc = v - mu
                var = jnp.mean(vc * vc, axis=-1, keepdims=True)
                y_ref[...] = (vc * lax.rsqrt(var + EPS) * lg_ref[...] + lb_ref[...]).astype(BF16)
            u = _gelu(_dot(hn_ref[...], win_ref[...]))
            row = lax.broadcasted_iota(jnp.int32, (cc, cc), 0)
            col = lax.broadcasted_iota(jnp.int32, (cc, cc), 1)
            bs = bs_ref[...]
            for gl in range(gpb):
                gi = blk * gpb + gl
                ws = jnp.where(row >= col, ws_ref[gi], 0.0).astype(BF16)
                bias = bs[:, gi:gi + 1]
                cs = slice(gi * C_GROUP_DIM, (gi + 1) * C_GROUP_DIM)
                us = slice(gl * C_GROUP_DIM, (gl + 1) * C_GROUP_DIM)
                for n in range(ODD_TM // cc):
                    rs = slice(n * cc, (n + 1) * cc)
                    mixed = _dot(ws, y_ref[rs, cs]) + bias
                    gated_ref[rs, cs] = (u[rs, us] * mixed).astype(BF16)

    @pl.when(j == 2 * ODD_NV)
    def _():
        o_ref[...] = x_ref[...] + _dot(gated_ref[...], wout_ref[...])


def _odd_mixer(x, g, w_in3, ln_g, ln_b, w_s, b_s_t, w_out3):
    n, d = x.shape
    tm, tn, nv = ODD_TM, ODD_TN, ODD_NV
    win_blk = lambda i, j: (0, 0, jnp.where(j < nv, j + nv, jnp.minimum(j - nv, nv - 1)))
    const = lambda *shape: pl.BlockSpec(shape, lambda i, j: (0,) * len(shape))
    return pl.pallas_call(
        _odd_mixer_kernel,
        out_shape=jax.ShapeDtypeStruct((n, d), F32),
        grid=(n // tm, 2 * nv + 1),
        in_specs=[pl.BlockSpec((tm, d), lambda i, j: (i, 0)),
                  const(1, d),
                  pl.BlockSpec((None, d, tn), win_blk),
                  const(1, d), const(1, d),
                  const(C_GROUPS, C_CHUNK, C_CHUNK),
                  const(C_CHUNK, C_GROUPS),
                  pl.BlockSpec((None, d, d), lambda i, j: (0, 0, 0))],
        out_specs=pl.BlockSpec((tm, d), lambda i, j: (i, 0)),
        scratch_shapes=[pltpu.VMEM((tm, d), BF16),
                        pltpu.VMEM((nv, tm, tn), F32),
                        pltpu.VMEM((tm, d), BF16),
                        pltpu.VMEM((tm, d), BF16)],
        compiler_params=_params(("parallel", "arbitrary"), 56),
        name="odd_mixer",
    )(x, g, w_in3, ln_g, ln_b, w_s, b_s_t, w_out3)


def kernel(x, even_norm, even_w_in, even_conv, even_a_log, even_dt_bias, even_sinks,
           even_onorm, even_w_out, odd_norm, odd_w_in, odd_ln_g, odd_ln_b, odd_w_s,
           odd_b_s, odd_w_out, ffn_norm, ffn_w_gate, ffn_w_up, ffn_w_down, final_norm):
    batch, seq, d = x.shape
    n = batch * seq
    h = x.reshape(n, d)
    row = lambda a: a.reshape(1, -1).astype(F32)

    w_bg = even_w_in[0, :, EVEN_MAIN:]
    w_gate = jnp.pad(jnp.concatenate([w_bg, w_bg[:, B_HEADS:]], axis=1),
                     ((0, 0), (0, LANES - GATE_ROWS))).astype(BF16)
    proj, gates = _norm_proj_even(h, row(even_norm[0]), even_w_in[0].astype(BF16), w_gate)
    out_a = _swa(proj, even_sinks[0].astype(F32), batch, seq)

    def gate_row(a):
        a = a.astype(F32)
        return jnp.pad(jnp.concatenate([a, a]), (GATE_GAM, LANES - GATE_ROWS)).reshape(1, LANES)

    out_b, (w_eo, wg0, wu0, wd0) = _gdn(
        proj, gates, even_conv[0], gate_row(even_a_log[0]), gate_row(even_dt_bias[0]),
        row(even_onorm[0]), batch, seq,
        [(even_w_out, 0), (ffn_w_gate, 0), (ffn_w_up, 0), (ffn_w_down, 0)])
    h, (w_oi, w_oo) = _proj_residual([out_a, out_b], w_eo[None], h, [(odd_w_in, 0), (odd_w_out, 0)])
    h, (wg1, wu1, wd1) = _ffn(h, row(ffn_norm[0]), wg0, wu0, wd0, row(final_norm),
                              [(ffn_w_gate, 1), (ffn_w_up, 1), (ffn_w_down, 1)], final_norm=False)

    h = _odd_mixer(h, row(odd_norm[0]), w_oi[None], row(odd_ln_g[0]), row(odd_ln_b[0]),
                   odd_w_s[0], odd_b_s[0].T, w_oo[None])
    h, _ = _ffn(h, row(ffn_norm[1]), wg1, wu1, wd1, row(final_norm), [], final_norm=True)
    return h.reshape(batch, seq, d)
```

```python
import functools

import jax
import jax.numpy as jnp
from jax import lax
from jax.experimental import pallas as pl
from jax.experimental.pallas import tpu as pltpu

F32 = jnp.float32
BF16 = jnp.bfloat16

D_MODEL = 2048
A_HEADS = 16
A_KV_HEADS = 2
A_HEAD_DIM = 64
WINDOW = 128
B_HEADS = 8
B_HEAD_DIM = 128
CONV_K = 4
DN_CHUNK = 64
DN_CHUNK_LOG2 = 6
C_GROUPS = 8
C_CHUNK = 128
C_GROUP_DIM = D_MODEL // C_GROUPS
EPS = 1e-6

LANES = 128
SUBLANES = 8
A_Q = A_HEADS * A_HEAD_DIM
A_KV = A_KV_HEADS * A_HEAD_DIM
B_W = B_HEADS * B_HEAD_DIM
EVEN_MAIN = A_Q + 2 * A_KV + 4 * B_W
BLK_KA = A_Q // LANES
BLK_VA = BLK_KA + A_KV // LANES
BLK_QB = BLK_VA + A_KV // LANES
BLK_KB = BLK_QB + B_HEADS
BLK_VB = BLK_KB + B_HEADS
BLK_Z = BLK_VB + B_HEADS
Z_SPLIT = 2 * LANES
GATE_BETA = 0
GATE_GAM = B_HEADS
GATE_LAST = 2 * B_HEADS
GATE_ROWS = 3 * B_HEADS

MIB = 1024 * 1024


def _params(sem, vmem_mib):
    return pltpu.CompilerParams(dimension_semantics=sem, vmem_limit_bytes=vmem_mib * MIB)


def _cast_specs(w, layer, nr, nc, tile_of):
    rows, cols = w.shape[1] // nr, w.shape[2] // nc
    assert w.shape[1] % nr == 0 and w.shape[2] % nc == 0
    assert rows % (2 * SUBLANES) == 0 and cols % LANES == 0
    return (pl.BlockSpec((None, rows, cols), lambda *g: (layer, *tile_of(*g))),
            pl.BlockSpec((rows, cols), lambda *g: tuple(tile_of(*g))),
            jax.ShapeDtypeStruct(w.shape[1:], BF16))


def _cast_tiles(srcs, dsts):
    for src, dst in zip(srcs, dsts):
        dst[...] = src[...].astype(dst.dtype)


def _dot(a, b):
    return jnp.dot(a, b, preferred_element_type=F32)


def _dot_nt(a, b):
    return lax.dot_general(a, b, (((1,), (1,)), ((), ())), preferred_element_type=F32)


def _rms_rows(x, g):
    ms = jnp.mean(x * x, axis=-1, keepdims=True)
    return x * lax.rsqrt(ms + EPS) * g


def _silu(x):
    return x * jax.nn.sigmoid(x)


def _gelu(x):
    return x * (lax.erf(x * (2.0 ** -0.5)) + 1.0) * 0.5


def _norm_proj_kernel(x_ref, g_ref, w_ref, wg_ref, o_ref, og_ref, hn_ref):
    @pl.when(pl.program_id(1) == 0)
    def _():
        hn = _rms_rows(x_ref[...], g_ref[...]).astype(BF16)
        hn_ref[...] = hn
        og_ref[...] = _dot(hn, wg_ref[...])

    o_ref[...] = _dot(hn_ref[...], w_ref[...])


def _norm_proj_even(x, g, w_in, w_gate, *, tm=1024, tn=1792):
    n, d = x.shape
    nout = EVEN_MAIN
    assert nout % tn == 0
    return pl.pallas_call(
        _norm_proj_kernel,
        out_shape=(jax.ShapeDtypeStruct((n, nout), F32),
                   jax.ShapeDtypeStruct((n, LANES), F32)),
        grid=(n // tm, nout // tn),
        in_specs=[pl.BlockSpec((tm, d), lambda i, j: (i, 0)),
                  pl.BlockSpec((1, d), lambda i, j: (0, 0)),
                  pl.BlockSpec((d, tn), lambda i, j: (0, j)),
                  pl.BlockSpec((d, LANES), lambda i, j: (0, 0))],
        out_specs=(pl.BlockSpec((tm, tn), lambda i, j: (i, j)),
                   pl.BlockSpec((tm, LANES), lambda i, j: (i, 0))),
        scratch_shapes=[pltpu.VMEM((tm, d), BF16)],
        compiler_params=_params(("parallel", "arbitrary"), 56),
        name="norm_proj_even",
    )(x, g, w_in, w_gate)


def _proj_res_kernel(*refs, n_in, n_cast):
    a_refs = refs[:n_in]
    w_refs = refs[n_in:2 * n_in]
    res_ref = refs[2 * n_in]
    cast_in = refs[2 * n_in + 1:2 * n_in + 1 + n_cast]
    o_ref = refs[2 * n_in + 1 + n_cast]
    cast_out = refs[2 * n_in + 2 + n_cast:]
    _cast_tiles(cast_in, cast_out)
    acc = res_ref[...]
    for a_ref, w_ref in zip(a_refs, w_refs):
        acc = acc + _dot(a_ref[...], w_ref[...])
    o_ref[...] = acc


def _proj_residual(a_list, w3, res, cast_list, *, tm=512):
    n, d = res.shape
    n_in = len(a_list)
    kk = a_list[0].shape[1]
    assert all(a.shape[1] == kk for a in a_list) and w3.shape[1] == n_in * kk
    ni = n // tm
    cast = [_cast_specs(w, layer, ni, 1, lambda i: (i, 0)) for w, layer in cast_list]
    in_specs = ([pl.BlockSpec((tm, kk), lambda i: (i, 0)) for _ in a_list]
                + [pl.BlockSpec((None, kk, d), lambda i, k=k: (0, k, 0)) for k in range(n_in)]
                + [pl.BlockSpec((tm, d), lambda i: (i, 0))] + [c[0] for c in cast])
    res_all = pl.pallas_call(
        functools.partial(_proj_res_kernel, n_in=n_in, n_cast=len(cast)),
        out_shape=(jax.ShapeDtypeStruct((n, d), F32), *[c[2] for c in cast]),
        grid=(ni,),
        in_specs=in_specs,
        out_specs=(pl.BlockSpec((tm, d), lambda i: (i, 0)), *[c[1] for c in cast]),
        compiler_params=_params(("parallel",), 40),
        name="proj_residual",
    )(*a_list, *([w3] * n_in), res, *[w for w, _ in cast_list])
    return res_all[0], list(res_all[1:])


def _ffn_kernel(x_ref, g_ref, wg_ref, wu_ref, wd_ref, fg_ref, *rest, final_norm, n_cast):
    cast_in = rest[:n_cast]
    o_ref = rest[n_cast]
    cast_out = rest[n_cast + 1:2 * n_cast + 1]
    hn_ref = rest[2 * n_cast + 1]
    f = pl.program_id(1)
    _cast_tiles(cast_in, cast_out)

    @pl.when(f == 0)
    def _():
        x = x_ref[...]
        hn_ref[...] = _rms_rows(x, g_ref[...]).astype(BF16)
        o_ref[...] = x

    hn = hn_ref[...]
    a = (_silu(_dot(hn, wg_ref[...])) * _dot(hn, wu_ref[...])).astype(BF16)
    o_ref[...] += _dot(a, wd_ref[...])

    if final_norm:
        @pl.when(f == pl.num_programs(1) - 1)
        def _():
            o_ref[...] = _rms_rows(o_ref[...], fg_ref[...])


def _ffn(x, g, w_gate, w_up, w_down, final_g, cast_list, *, final_norm, tm=1024, tf=512):
    n, d = x.shape
    dff = w_gate.shape[1]
    ni, nf = n // tm, dff // tf
    cast = [_cast_specs(w, layer, ni, nf, lambda i, f: (i, f)) if w.shape[1] == d else
            _cast_specs(w, layer, nf, ni, lambda i, f: (f, i)) for w, layer in cast_list]
    res = pl.pallas_call(
        functools.partial(_ffn_kernel, final_norm=final_norm, n_cast=len(cast)),
        out_shape=(jax.ShapeDtypeStruct((n, d), F32), *[c[2] for c in cast]),
        grid=(ni, nf),
        in_specs=[pl.BlockSpec((tm, d), lambda i, f: (i, 0)),
                  pl.BlockSpec((1, d), lambda i, f: (0, 0)),
                  pl.BlockSpec((d, tf), lambda i, f: (0, f)),
                  pl.BlockSpec((d, tf), lambda i, f: (0, f)),
                  pl.BlockSpec((tf, d), lambda i, f: (f, 0)),
                  pl.BlockSpec((1, d), lambda i, f: (0, 0)), *[c[0] for c in cast]],
        out_specs=(pl.BlockSpec((tm, d), lambda i, f: (i, 0)), *[c[1] for c in cast]),
        scratch_shapes=[pltpu.VMEM((tm, d), BF16)],
        compiler_params=_params(("parallel", "arbitrary"), 58 if cast else 56),
        name="swiglu_ffn",
    )(x, g, w_gate, w_up, w_down, final_g, *[w for w, _ in cast_list])
    return res[0], list(res[1:])


SCAN_TB = 512
SCAN_NC = SCAN_TB // DN_CHUNK
SWA_SUB = SCAN_TB // WINDOW


def _swa_kernel(sink_ref, q_ref, k_ref, v_ref, kp_ref, vp_ref, o_ref):
    first = pl.program_id(1) == 0
    w = WINDOW
    grp = A_HEADS // A_KV_HEADS
    lane = lax.broadcasted_iota(jnp.int32, (w, LANES), 1)
    low = lane < A_HEAD_DIM
    row = lax.broadcasted_iota(jnp.int32, (w, w), 0)
    col = lax.broadcasted_iota(jnp.int32, (w, w), 1)
    cur_ok = jnp.concatenate([col <= row] * 2, axis=0)

    def dup_half(x, head):
        swapped = pltpu.roll(x, A_HEAD_DIM, 1)
        if head == 0:
            return jnp.where(low, x, swapped)
        return jnp.where(low, swapped, x)

    kvs = [(kp_ref[...], vp_ref[...])] + [(k_ref[sb * w:(sb + 1) * w, :], v_ref[sb * w:(sb + 1) * w, :])
                                           for sb in range(SWA_SUB)]
    dups = [[(dup_half(kk, kvh), dup_half(vv, kvh)) for kk, vv in kvs] for kvh in range(A_KV_HEADS)]
    k2s = [[jnp.concatenate([d[sb + 1][0], d[sb][0]], axis=0).astype(BF16) for sb in range(SWA_SUB)]
           for d in dups]
    v2s = [[jnp.concatenate([d[sb + 1][1], d[sb][1]], axis=0).astype(BF16) for sb in range(SWA_SUB)]
           for d in dups]
    scale = A_HEAD_DIM ** -0.5
    zero = jnp.zeros((), BF16)
    chains = [(sb, pr) for sb in range(SWA_SUB) for pr in range(A_HEADS // 2)]
    s2s, sinks = [], []
    for sb, pr in chains:
        pair = q_ref[sb * w:(sb + 1) * w, pr * LANES:(pr + 1) * LANES] * scale
        qst = jnp.concatenate([jnp.where(low, pair, 0.0), jnp.where(low, 0.0, pair)], axis=0)
        s2s.append(_dot_nt(qst.astype(BF16), k2s[2 * pr // grp][sb]))
        sinks.append(jnp.concatenate([jnp.full((w, w), sink_ref[2 * pr], F32),
                                      jnp.full((w, w), sink_ref[2 * pr + 1], F32)], axis=0))
    p2s = []
    for (sb, pr), s2, sink in zip(chains, s2s, sinks):
        prev = jnp.where(first, -jnp.inf, s2[:, w:]) if sb == 0 else s2[:, w:]
        s = jnp.where(cur_ok, s2[:, :w], prev)
        m = jnp.maximum(jnp.max(s, axis=-1, keepdims=True), sink)
        p = jnp.exp(s - m)
        denom = jnp.sum(p, axis=-1, keepdims=True) + jnp.exp(sink - m)
        pb = (p * (1.0 / denom)).astype(BF16)
        p2s.append(jnp.concatenate([jnp.where(cur_ok, pb, zero), jnp.where(cur_ok, zero, pb)], axis=-1))
    for (sb, pr), p2 in zip(chains, p2s):
        o = _dot(p2, v2s[2 * pr // grp][sb])
        o_ref[sb * w:(sb + 1) * w, pr * LANES:(pr + 1) * LANES] = (
            jnp.where(low, o[:w], o[w:]).astype(o_ref.dtype))


def _swa_specs():
    tq = SWA_SUB * WINDOW
    spec = lambda blk, prev: (
        pl.BlockSpec((None, WINDOW, LANES), lambda b, n: (b, jnp.maximum(n * SWA_SUB - 1, 0), blk))
        if prev else pl.BlockSpec((None, tq, LANES), lambda b, n: (b, n, blk)))
    return [pl.BlockSpec(memory_space=pltpu.SMEM),
            pl.BlockSpec((None, tq, A_Q), lambda b, n: (b, n, 0)),
            spec(BLK_KA, False), spec(BLK_VA, False), spec(BLK_KA, True), spec(BLK_VA, True)]


GATE_TB = 512
GDN_TB = 2048
GDN_NC = GDN_TB // DN_CHUNK
GDN_GRP = 128
GRP_NC = GDN_GRP // DN_CHUNK


def _split3(x):
    hi = x.astype(BF16)
    r1 = x - hi.astype(F32)
    mid = r1.astype(BF16)
    lo = (r1 - mid.astype(F32)).astype(BF16)
    return hi, mid, lo


def _gate_prep_kernel(gate_ref, alog_ref, dt_ref, o_ref, ot_ref):
    tb = GATE_TB
    gate = gate_ref[...]
    lane = lax.broadcasted_iota(jnp.int32, (tb, LANES), 1)
    beta = jax.nn.sigmoid(gate)
    g = -jnp.exp(alog_ref[...]) * jax.nn.softplus(gate + dt_ref[...])
    row = lax.broadcasted_iota(jnp.int32, (tb, tb), 0)
    col = lax.broadcasted_iota(jnp.int32, (tb, tb), 1)
    same = (row >> DN_CHUNK_LOG2) == (col >> DN_CHUNK_LOG2)
    tri = jnp.where(jnp.logical_and(same, row >= col), 1.0, 0.0).astype(BF16)
    ones = jnp.where(same, 1.0, 0.0).astype(BF16)
    r = _dot(jnp.concatenate([tri, ones], axis=0), jnp.concatenate(_split3(g), axis=-1))
    s = r[:, :LANES] + r[:, LANES:2 * LANES] + r[:, 2 * LANES:]
    out = jnp.where(lane < GATE_GAM, beta, jnp.where(lane < GATE_LAST, s[:tb], s[tb:]))
    o_ref[...] = out
    ot_ref[...] = out.T[:GATE_ROWS, :]


def _gate_prep(gates, alog_row, dt_row):
    n = gates.shape[0]
    tb = GATE_TB
    row = pl.BlockSpec((1, LANES), lambda i: (0, 0))
    return pl.pallas_call(
        _gate_prep_kernel,
        out_shape=(jax.ShapeDtypeStruct((n, LANES), F32),
                   jax.ShapeDtypeStruct((GATE_ROWS, n), F32)),
        grid=(n // tb,),
        in_specs=[pl.BlockSpec((tb, LANES), lambda i: (i, 0)), row, row],
        out_specs=(pl.BlockSpec((tb, LANES), lambda i: (i, 0)),
                   pl.BlockSpec((GATE_ROWS, tb), lambda i: (0, i))),
        compiler_params=_params(("parallel",), 8),
        name="gdn_gate_prep",
    )(gates, alog_row, dt_row)


def _conv_silu(x_ref, tail, w, buf_ref):
    t = x_ref.shape[0]
    x = x_ref[...]
    buf_ref[0:SUBLANES, :] = tail
    buf_ref[SUBLANES:, :] = x
    acc = x * w[CONV_K - 1:CONV_K, :]
    for j in range(CONV_K - 1):
        acc = acc + buf_ref[pl.ds(SUBLANES - (CONV_K - 1 - j), t), :] * w[j:j + 1, :]
    return _silu(acc)


def _lane_col(x, idx):
    lane = lax.broadcasted_iota(jnp.int32, x.shape, 1)
    return jnp.sum(jnp.where(lane == idx, x, 0.0), axis=-1, keepdims=True)


def _gdn_pre_kernel(q_ref, k_ref, v_ref, hq_ref, hk_ref, hv_ref, gc_ref, gr_ref,
                    cq_ref, ck_ref, cv_ref, *rest, n_cast, n_cast_step):
    cast_in = rest[:n_cast]
    kw_ref, nn_ref, qe_ref, ol_ref, gl_ref = rest[n_cast:n_cast + 5]
    cast_out = rest[n_cast + 5:2 * n_cast + 5]
    buf_ref = rest[2 * n_cast + 5]
    h = pl.program_id(1)
    first = pl.program_id(2) == 0
    gs = GDN_GRP

    _cast_tiles(cast_in[:n_cast_step], cast_out[:n_cast_step])

    @pl.when(first)
    def _():
        _cast_tiles(cast_in[n_cast_step:], cast_out[n_cast_step:])

    def conv(x_ref, halo_ref, w_ref, slot):
        tail = jnp.where(first, 0.0, halo_ref[...])
        return _conv_silu(x_ref, tail, w_ref[...], buf_ref.at[slot])

    q = conv(q_ref, hq_ref, cq_ref, 0)
    k = conv(k_ref, hk_ref, ck_ref, 1)
    v = conv(v_ref, hv_ref, cv_ref, 2)
    q = q * lax.rsqrt(jnp.sum(q * q, axis=-1, keepdims=True) + EPS) * (B_HEAD_DIM ** -0.5)
    k = k * lax.rsqrt(jnp.sum(k * k, axis=-1, keepdims=True) + EPS)

    gall = gc_ref[...]
    beta = _lane_col(gall, h + GATE_BETA)
    gam = _lane_col(gall, h + GATE_GAM)
    gam_end = _lane_col(gall, h + GATE_LAST)
    gam_row = gr_ref[pl.ds(h + GATE_GAM, 1), :]
    egam = jnp.exp(gam)
    q_dec = q * egam
    k_dec = k * jnp.exp(gam_end - gam)
    rhs_all = jnp.concatenate([v * beta, k * (beta * egam)], axis=-1)
    ends = gc_ref[pl.ds(DN_CHUNK - 1, GDN_NC, stride=DN_CHUNK), :]
    gl_ref[...] = jnp.broadcast_to(jnp.exp(_lane_col(ends, h + GATE_GAM)), (GDN_NC, LANES))

    row = lax.broadcasted_iota(jnp.int32, (gs, gs), 0)
    col = lax.broadcasted_iota(jnp.int32, (gs, gs), 1)
    same = (row >> DN_CHUNK_LOG2) == (col >> DN_CHUNK_LOG2)
    incl = jnp.logical_and(same, row >= col)
    strict = jnp.logical_and(same, row > col)
    trow = lax.broadcasted_iota(jnp.int32, (GRP_NC * B_HEAD_DIM, gs), 0)
    tcol = lax.broadcasted_iota(jnp.int32, (GRP_NC * B_HEAD_DIM, gs), 1)
    own = (trow >> 7) == (tcol >> DN_CHUNK_LOG2)

    groups = range(GDN_TB // gs)
    sls = [slice(grp * gs, (grp + 1) * gs) for grp in groups]
    kbs = [k[sl].astype(BF16) for sl in sls]
    scs = [_dot_nt(jnp.concatenate([q[sl].astype(BF16), kb], axis=0), kb)
           for sl, kb in zip(sls, kbs)]
    decays = [jnp.exp(jnp.where(incl, gam[sl] - gam_row[:, sl], -jnp.inf)) for sl in sls]
    qks = [(sc[:gs] * dec).astype(BF16) for sc, dec in zip(scs, decays)]
    xs = [jnp.where(strict, (-beta[sl]) * sc[gs:] * dec, 0.0)
          for sl, sc, dec in zip(sls, scs, decays)]
    ys = [rhs_all[sl] for sl in sls]
    for _ in range(DN_CHUNK_LOG2 - 1):
        xbs = [x.astype(BF16) for x in xs]
        rs = [_dot(xb, jnp.concatenate([xb, y.astype(BF16)], axis=-1))
              for xb, y in zip(xbs, ys)]
        xs = [r[:, :gs] for r in rs]
        ys = [y + r[:, gs:] for y, r in zip(ys, rs)]
    ys = [y + _dot(x.astype(BF16), y.astype(BF16)) for x, y in zip(xs, ys)]
    ybs = [y.astype(BF16) for y in ys]
    es = [_dot(qk, yb) for qk, yb in zip(qks, ybs)]
    kdt4s = [jnp.where(own, jnp.concatenate([k_dec[sl].T] * GRP_NC, axis=0), 0.0).astype(BF16)
             for sl in sls]
    fs = [_dot(kdt4, yb) for kdt4, yb in zip(kdt4s, ybs)]
    for grp, sl, e, f in zip(groups, sls, es, fs):
        ol_ref[sl, :] = e[:, :B_HEAD_DIM]
        qe_ref[sl, :] = (q_dec[sl] - e[:, B_HEAD_DIM:]).astype(qe_ref.dtype)
        rws = slice(grp * GRP_NC * B_HEAD_DIM, (grp + 1) * GRP_NC * B_HEAD_DIM)
        nn_ref[rws, :] = f[:, :B_HEAD_DIM]
        kw_ref[rws, :] = f[:, B_HEAD_DIM:].astype(kw_ref.dtype)


def _gdn_scan_kernel(kw_ref, nn_ref, qe_ref, ol_ref, gl_ref, *rest):
    nz = B_W // Z_SPLIT
    z_refs = rest[:nz]
    onorm_ref, o_ref, state_ref = rest[nz:]
    c = DN_CHUNK
    dk = B_HEAD_DIM
    hpz = Z_SPLIT // dk

    @pl.when(pl.program_id(1) == 0)
    def _():
        state_ref[...] = jnp.zeros_like(state_ref)

    onorm = onorm_ref[...]
    for n in range(SCAN_NC):
        ts = slice(n * c, (n + 1) * c)
        ds = slice(n * dk, (n + 1) * dk)
        for h in range(B_HEADS):
            hs = slice(h * dk, (h + 1) * dk)
            s = state_ref[h]
            r = _dot(jnp.concatenate([kw_ref[h, ds, :], qe_ref[h, ts, :]], axis=0), s.astype(BF16))
            state_ref[h] = s * gl_ref[h, n:n + 1, :] + nn_ref[h, ds, :] - r[:dk]
            o = ol_ref[h, ts, :] + r[dk:]
            o = o * lax.rsqrt(jnp.mean(o * o, axis=-1, keepdims=True) + EPS) * onorm
            z = z_refs[h // hpz][ts, (h % hpz) * dk:(h % hpz + 1) * dk]
            o_ref[ts, hs] = (o * _silu(z)).astype(o_ref.dtype)


def _gdn_pre(proj, gates, conv_w, alog_row, dt_row, batch, seq, cast_list):
    gcol, grow = _gate_prep(gates, alog_row, dt_row)
    tb = GDN_TB
    nt = seq // tb
    bf16_rows = 2 * SUBLANES
    n_steps = batch * B_HEADS * nt
    per_step = lambda wl: wl[0].shape[1] % (n_steps * bf16_rows) == 0
    order = sorted(range(len(cast_list)), key=lambda idx: not per_step(cast_list[idx]))
    cast_list = [cast_list[idx] for idx in order]
    n_cast_step = sum(per_step(wl) for wl in cast_list)
    cast = [_cast_specs(w, layer, n_steps, 1, lambda b, h, t: ((b * B_HEADS + h) * nt + t, 0))
            if per_step((w, layer)) else
            _cast_specs(w, layer, batch * B_HEADS, 1, lambda b, h, t: (b * B_HEADS + h, 0))
            for w, layer in cast_list]
    cast_in_specs, cast_out_specs, cast_shapes = ([c[k] for c in cast] for k in range(3))
    proj3 = proj.reshape(batch, seq, proj.shape[-1])
    gcol3 = gcol.reshape(batch, seq, LANES)
    nch = seq // DN_CHUNK
    dk = B_HEAD_DIM
    tok = lambda blk: pl.BlockSpec((None, tb, LANES), lambda b, h, t: (b, t, blk + h))
    halo = lambda blk: pl.BlockSpec(
        (None, SUBLANES, LANES),
        lambda b, h, t: (b, jnp.maximum(t * (tb // SUBLANES) - 1, 0), blk + h))
    cw = lambda blk: pl.BlockSpec((CONV_K, LANES), lambda b, h, t: (0, blk + h))
    per_chunk = lambda rows: pl.BlockSpec((None, None, rows, LANES), lambda b, h, t: (b, h, t, 0))
    kw, nn, qe, ol, gl, *casts = pl.pallas_call(
        functools.partial(_gdn_pre_kernel, n_cast=len(cast_list), n_cast_step=n_cast_step),
        out_shape=(jax.ShapeDtypeStruct((batch, B_HEADS, nch * dk, dk), BF16),
                   jax.ShapeDtypeStruct((batch, B_HEADS, nch * dk, dk), F32),
                   jax.ShapeDtypeStruct((batch, B_HEADS, seq, dk), BF16),
                   jax.ShapeDtypeStruct((batch, B_HEADS, seq, dk), F32),
                   jax.ShapeDtypeStruct((batch, B_HEADS, nch, LANES), F32),
                   *cast_shapes),
        grid=(batch, B_HEADS, nt),
        in_specs=[tok(BLK_QB), tok(BLK_KB), tok(BLK_VB),
                  halo(BLK_QB), halo(BLK_KB), halo(BLK_VB),
                  pl.BlockSpec((None, tb, LANES), lambda b, h, t: (b, t, 0)),
                  pl.BlockSpec((GATE_ROWS, tb), lambda b, h, t: (0, b * nt + t)),
                  cw(0), cw(B_HEADS), cw(2 * B_HEADS), *cast_in_specs],
        out_specs=(per_chunk(GDN_NC * dk), per_chunk(GDN_NC * dk), per_chunk(tb), per_chunk(tb),
                   per_chunk(GDN_NC), *cast_out_specs),
        scratch_shapes=[pltpu.VMEM((3, SUBLANES + tb, LANES), F32)],
        compiler_params=_params(("parallel", "parallel", "arbitrary"), 46),
        name="gdn_pre",
    )(proj3, proj3, proj3, proj3, proj3, proj3, gcol3, grow, conv_w, conv_w, conv_w,
      *[w for w, _ in cast_list])

    return (kw, nn, qe, ol, gl), [casts[order.index(idx)] for idx in range(len(order))]


def _mixers_tail_kernel(sink_ref, q_ref, k_ref, v_ref, kp_ref, vp_ref, *rest):
    *scan_in, oa_ref, ob_ref, state_ref = rest
    _swa_kernel(sink_ref, q_ref, k_ref, v_ref, kp_ref, vp_ref, oa_ref)
    _gdn_scan_kernel(*scan_in, ob_ref, state_ref)


def _mixers_tail(proj, sinks, pre, onorm, batch, seq):
    kw, nn, qe, ol, gl = pre
    proj3 = proj.reshape(batch, seq, proj.shape[-1])
    dk = B_HEAD_DIM
    heads = lambda rows: pl.BlockSpec((None, B_HEADS, rows, LANES), lambda b, t: (b, 0, t, 0))
    ts = SCAN_TB
    nz = B_W // Z_SPLIT
    assert (BLK_Z * LANES) % Z_SPLIT == 0
    z_blk0 = BLK_Z * LANES // Z_SPLIT
    out_a, out_b = pl.pallas_call(
        _mixers_tail_kernel,
        out_shape=(jax.ShapeDtypeStruct((batch, seq, A_Q), BF16),
                   jax.ShapeDtypeStruct((batch, seq, B_W), BF16)),
        grid=(batch, seq // ts),
        in_specs=_swa_specs()
        + [heads(SCAN_NC * dk), heads(SCAN_NC * dk), heads(ts), heads(ts), heads(SCAN_NC)]
        + [pl.BlockSpec((None, ts, Z_SPLIT), lambda b, t, k=k: (b, t, z_blk0 + k)) for k in range(nz)]
        + [pl.BlockSpec((1, dk), lambda b, t: (0, 0))],
        out_specs=(pl.BlockSpec((None, ts, A_Q), lambda b, t: (b, t, 0)),
                   pl.BlockSpec((None, ts, B_W), lambda b, t: (b, t, 0))),
        scratch_shapes=[pltpu.VMEM((B_HEADS, dk, dk), F32)],
        compiler_params=_params(("parallel", "arbitrary"), 48),
        name="mixers_tail",
    )(sinks, *([proj3] * 5), kw, nn, qe, ol, gl, *([proj3] * nz), onorm)
    return out_a.reshape(batch * seq, A_Q), out_b.reshape(batch * seq, B_W)


ODD_TM = 512
ODD_TN = 1024
ODD_NV = D_MODEL // ODD_TN


def _odd_mixer_kernel(x_ref, g_ref, win_ref, lg_ref, lb_ref, ws_ref, bs_ref, wout_ref,
                      o_ref, hn_ref, v_ref, y_ref, gated_ref):
    j = pl.program_id(1)
    cc = C_CHUNK
    gpb = ODD_TN // C_GROUP_DIM

    @pl.when(j == 0)
    def _():
        hn_ref[...] = _rms_rows(x_ref[...], g_ref[...]).astype(BF16)

    @pl.when(j < ODD_NV)
    def _():
        v_ref[j] = _gelu(_dot(hn_ref[...], win_ref[...]))

    for blk in range(ODD_NV):
        @pl.when(j == ODD_NV + blk)
        def _(blk=blk):
            if blk == 0:
                v = jnp.concatenate([v_ref[k] for k in range(ODD_NV)], axis=-1)
                mu = jnp.mean(v, axis=-1, keepdims=True)
                vc = v - mu
                var = jnp.mean(vc * vc, axis=-1, keepdims=True)
                y_ref[...] = (vc * lax.rsqrt(var + EPS) * lg_ref[...] + lb_ref[...]).astype(BF16)
            u = _gelu(_dot(hn_ref[...], win_ref[...]))
            row = lax.broadcasted_iota(jnp.int32, (cc, cc), 0)
            col = lax.broadcasted_iota(jnp.int32, (cc, cc), 1)
            bs = bs_ref[...]
            for gl in range(gpb):
                gi = blk * gpb + gl
                ws = jnp.where(row >= col, ws_ref[gi], 0.0).astype(BF16)
                bias = bs[:, gi:gi + 1]
                cs = slice(gi * C_GROUP_DIM, (gi + 1) * C_GROUP_DIM)
                us = slice(gl * C_GROUP_DIM, (gl + 1) * C_GROUP_DIM)
                for n in range(ODD_TM // cc):
                    rs = slice(n * cc, (n + 1) * cc)
                    mixed = _dot(ws, y_ref[rs, cs]) + bias
                    gated_ref[rs, cs] = (u[rs, us] * mixed).astype(BF16)

    @pl.when(j == 2 * ODD_NV)
    def _():
        o_ref[...] = x_ref[...] + _dot(gated_ref[...], wout_ref[...])


def _odd_mixer(x, g, w_in3, ln_g, ln_b, w_s, b_s_t, w_out3):
    n, d = x.shape
    tm, tn, nv = ODD_TM, ODD_TN, ODD_NV
    win_blk = lambda i, j: (0, 0, jnp.where(j < nv, j + nv, jnp.minimum(j - nv, nv - 1)))
    const = lambda *shape: pl.BlockSpec(shape, lambda i, j: (0,) * len(shape))
    return pl.pallas_call(
        _odd_mixer_kernel,
        out_shape=jax.ShapeDtypeStruct((n, d), F32),
        grid=(n // tm, 2 * nv + 1),
        in_specs=[pl.BlockSpec((tm, d), lambda i, j: (i, 0)),
                  const(1, d),
                  pl.BlockSpec((None, d, tn), win_blk),
                  const(1, d), const(1, d),
                  const(C_GROUPS, C_CHUNK, C_CHUNK),
                  const(C_CHUNK, C_GROUPS),
                  pl.BlockSpec((None, d, d), lambda i, j: (0, 0, 0))],
        out_specs=pl.BlockSpec((tm, d), lambda i, j: (i, 0)),
        scratch_shapes=[pltpu.VMEM((tm, d), BF16),
                        pltpu.VMEM((nv, tm, tn), F32),
                        pltpu.VMEM((tm, d), BF16),
                        pltpu.VMEM((tm, d), BF16)],
        compiler_params=_params(("parallel", "arbitrary"), 56),
        name="odd_mixer",
    )(x, g, w_in3, ln_g, ln_b, w_s, b_s_t, w_out3)


def kernel(x, even_norm, even_w_in, even_conv, even_a_log, even_dt_bias, even_sinks,
           even_onorm, even_w_out, odd_norm, odd_w_in, odd_ln_g, odd_ln_b, odd_w_s,
           odd_b_s, odd_w_out, ffn_norm, ffn_w_gate, ffn_w_up, ffn_w_down, final_norm):
    batch, seq, d = x.shape
    n = batch * seq
    h = x.reshape(n, d)
    row = lambda a: a.reshape(1, -1).astype(F32)

    w_bg = even_w_in[0, :, EVEN_MAIN:]
    w_gate = jnp.pad(jnp.concatenate([w_bg, w_bg[:, B_HEADS:]], axis=1),
                     ((0, 0), (0, LANES - GATE_ROWS))).astype(BF16)
    proj, gates = _norm_proj_even(h, row(even_norm[0]), even_w_in[0].astype(BF16), w_gate)
    def gate_row(a):
        a = a.astype(F32)
        return jnp.pad(jnp.concatenate([a, a]), (GATE_GAM, LANES - GATE_ROWS)).reshape(1, LANES)

    pre, (w_eo, wg0, wu0, wd0) = _gdn_pre(
        proj, gates, even_conv[0], gate_row(even_a_log[0]), gate_row(even_dt_bias[0]), batch, seq,
        [(even_w_out, 0), (ffn_w_gate, 0), (ffn_w_up, 0), (ffn_w_down, 0)])
    out_a, out_b = _mixers_tail(proj, even_sinks[0].astype(F32), pre, row(even_onorm[0]), batch, seq)
    h, (w_oi, w_oo) = _proj_residual([out_a, out_b], w_eo[None], h, [(odd_w_in, 0), (odd_w_out, 0)])
    h, (wg1, wu1, wd1) = _ffn(h, row(ffn_norm[0]), wg0, wu0, wd0, row(final_norm),
                              [(ffn_w_gate, 1), (ffn_w_up, 1), (ffn_w_down, 1)], final_norm=False)

    h = _odd_mixer(h, row(odd_norm[0]), w_oi[None], row(odd_ln_g[0]), row(odd_ln_b[0]),
                   odd_w_s[0], odd_b_s[0].T, w_oo[None])
    h, _ = _ffn(h, row(ffn_norm[1]), wg1, wu1, wd1, row(final_norm), [], final_norm=True)
    return h.reshape(batch, seq, d)
```

```python
import functools

import jax
import jax.numpy as jnp
from jax import lax
from jax.experimental import pallas as pl
from jax.experimental.pallas import tpu as pltpu

F32 = jnp.float32
BF16 = jnp.bfloat16

D_MODEL = 2048
A_HEADS = 16
A_KV_HEADS = 2
A_HEAD_DIM = 64
WINDOW = 128
B_HEADS = 8
B_HEAD_DIM = 128
CONV_K = 4
DN_CHUNK = 64
DN_CHUNK_LOG2 = 6
C_GROUPS = 8
C_CHUNK = 128
C_GROUP_DIM = D_MODEL // C_GROUPS
EPS = 1e-6

LANES = 128
SUBLANES = 8
A_Q = A_HEADS * A_HEAD_DIM
A_KV = A_KV_HEADS * A_HEAD_DIM
B_W = B_HEADS * B_HEAD_DIM
EVEN_MAIN = A_Q + 2 * A_KV + 4 * B_W
BLK_KA = A_Q // LANES
BLK_VA = BLK_KA + A_KV // LANES
BLK_QB = BLK_VA + A_KV // LANES
BLK_KB = BLK_QB + B_HEADS
BLK_VB = BLK_KB + B_HEADS
BLK_Z = BLK_VB + B_HEADS
Z_SPLIT = 2 * LANES
GATE_BETA = 0
GATE_GAM = B_HEADS
GATE_LAST = 2 * B_HEADS
GATE_ROWS = 3 * B_HEADS

MIB = 1024 * 1024


def _params(sem, vmem_mib):
    return pltpu.CompilerParams(dimension_semantics=sem, vmem_limit_bytes=vmem_mib * MIB)


def _cast_specs(w, layer, nr, nc, tile_of):
    rows, cols = w.shape[1] // nr, w.shape[2] // nc
    assert w.shape[1] % nr == 0 and w.shape[2] % nc == 0
    assert rows % (2 * SUBLANES) == 0 and cols % LANES == 0
    return (pl.BlockSpec((None, rows, cols), lambda *g: (layer, *tile_of(*g))),
            pl.BlockSpec((rows, cols), lambda *g: tuple(tile_of(*g))),
            jax.ShapeDtypeStruct(w.shape[1:], BF16))


def _cast_tiles(srcs, dsts):
    for src, dst in zip(srcs, dsts):
        dst[...] = src[...].astype(dst.dtype)


def _dot(a, b):
    return jnp.dot(a, b, preferred_element_type=F32)


def _dot_nt(a, b):
    return lax.dot_general(a, b, (((1,), (1,)), ((), ())), preferred_element_type=F32)


def _rms_rows(x, g):
    ms = jnp.mean(x * x, axis=-1, keepdims=True)
    return x * lax.rsqrt(ms + EPS) * g


def _silu(x):
    return x * jax.nn.sigmoid(x)


def _gelu(x):
    return x * (lax.erf(x * (2.0 ** -0.5)) + 1.0) * 0.5


def _norm_proj_kernel(x_ref, g_ref, w_ref, wg_ref, o_ref, og_ref, hn_ref):
    @pl.when(pl.program_id(1) == 0)
    def _():
        hn = _rms_rows(x_ref[...], g_ref[...]).astype(BF16)
        hn_ref[...] = hn
        og_ref[...] = _dot(hn, wg_ref[...])

    o_ref[...] = _dot(hn_ref[...], w_ref[...])


def _norm_proj_even(x, g, w_in, w_gate, *, tm=1024, tn=1792):
    n, d = x.shape
    nout = EVEN_MAIN
    assert nout % tn == 0
    return pl.pallas_call(
        _norm_proj_kernel,
        out_shape=(jax.ShapeDtypeStruct((n, nout), F32),
                   jax.ShapeDtypeStruct((n, LANES), F32)),
        grid=(n // tm, nout // tn),
        in_specs=[pl.BlockSpec((tm, d), lambda i, j: (i, 0)),
                  pl.BlockSpec((1, d), lambda i, j: (0, 0)),
                  pl.BlockSpec((d, tn), lambda i, j: (0, j)),
                  pl.BlockSpec((d, LANES), lambda i, j: (0, 0))],
        out_specs=(pl.BlockSpec((tm, tn), lambda i, j: (i, j)),
                   pl.BlockSpec((tm, LANES), lambda i, j: (i, 0))),
        scratch_shapes=[pltpu.VMEM((tm, d), BF16)],
        compiler_params=_params(("parallel", "arbitrary"), 56),
        name="norm_proj_even",
    )(x, g, w_in, w_gate)


def _proj_res_kernel(*refs, n_in, n_cast):
    a_refs = refs[:n_in]
    w_refs = refs[n_in:2 * n_in]
    res_ref = refs[2 * n_in]
    cast_in = refs[2 * n_in + 1:2 * n_in + 1 + n_cast]
    o_ref = refs[2 * n_in + 1 + n_cast]
    cast_out = refs[2 * n_in + 2 + n_cast:]
    _cast_tiles(cast_in, cast_out)
    acc = res_ref[...]
    for a_ref, w_ref in zip(a_refs, w_refs):
        acc = acc + _dot(a_ref[...], w_ref[...])
    o_ref[...] = acc


def _proj_residual(a_list, w3, res, cast_list, *, tm=512):
    n, d = res.shape
    n_in = len(a_list)
    kk = a_list[0].shape[1]
    assert all(a.shape[1] == kk for a in a_list) and w3.shape[1] == n_in * kk
    ni = n // tm
    cast = [_cast_specs(w, layer, ni, 1, lambda i: (i, 0)) for w, layer in cast_list]
    in_specs = ([pl.BlockSpec((tm, kk), lambda i: (i, 0)) for _ in a_list]
                + [pl.BlockSpec((None, kk, d), lambda i, k=k: (0, k, 0)) for k in range(n_in)]
                + [pl.BlockSpec((tm, d), lambda i: (i, 0))] + [c[0] for c in cast])
    res_all = pl.pallas_call(
        functools.partial(_proj_res_kernel, n_in=n_in, n_cast=len(cast)),
        out_shape=(jax.ShapeDtypeStruct((n, d), F32), *[c[2] for c in cast]),
        grid=(ni,),
        in_specs=in_specs,
        out_specs=(pl.BlockSpec((tm, d), lambda i: (i, 0)), *[c[1] for c in cast]),
        compiler_params=_params(("parallel",), 40),
        name="proj_residual",
    )(*a_list, *([w3] * n_in), res, *[w for w, _ in cast_list])
    return res_all[0], list(res_all[1:])


def _ffn_kernel(x_ref, g_ref, wg_ref, wu_ref, wd_ref, fg_ref, *rest, final_norm, n_cast):
    cast_in = rest[:n_cast]
    o_ref = rest[n_cast]
    cast_out = rest[n_cast + 1:2 * n_cast + 1]
    hn_ref = rest[2 * n_cast + 1]
    f = pl.program_id(1)
    _cast_tiles(cast_in, cast_out)

    @pl.when(f == 0)
    def _():
        x = x_ref[...]
        hn_ref[...] = _rms_rows(x, g_ref[...]).astype(BF16)
        o_ref[...] = x

    hn = hn_ref[...]
    a = (_silu(_dot(hn, wg_ref[...])) * _dot(hn, wu_ref[...])).astype(BF16)
    o_ref[...] += _dot(a, wd_ref[...])

    if final_norm:
        @pl.when(f == pl.num_programs(1) - 1)
        def _():
            o_ref[...] = _rms_rows(o_ref[...], fg_ref[...])


def _ffn(x, g, w_gate, w_up, w_down, final_g, cast_list, *, final_norm, tm=1024, tf=512):
    n, d = x.shape
    dff = w_gate.shape[1]
    ni, nf = n // tm, dff // tf
    cast = [_cast_specs(w, layer, ni, nf, lambda i, f: (i, f)) if w.shape[1] == d else
            _cast_specs(w, layer, nf, ni, lambda i, f: (f, i)) for w, layer in cast_list]
    res = pl.pallas_call(
        functools.partial(_ffn_kernel, final_norm=final_norm, n_cast=len(cast)),
        out_shape=(jax.ShapeDtypeStruct((n, d), F32), *[c[2] for c in cast]),
        grid=(ni, nf),
        in_specs=[pl.BlockSpec((tm, d), lambda i, f: (i, 0)),
                  pl.BlockSpec((1, d), lambda i, f: (0, 0)),
                  pl.BlockSpec((d, tf), lambda i, f: (0, f)),
                  pl.BlockSpec((d, tf), lambda i, f: (0, f)),
                  pl.BlockSpec((tf, d), lambda i, f: (f, 0)),
                  pl.BlockSpec((1, d), lambda i, f: (0, 0)), *[c[0] for c in cast]],
        out_specs=(pl.BlockSpec((tm, d), lambda i, f: (i, 0)), *[c[1] for c in cast]),
        scratch_shapes=[pltpu.VMEM((tm, d), BF16)],
        compiler_params=_params(("parallel", "arbitrary"), 58 if cast else 56),
        name="swiglu_ffn",
    )(x, g, w_gate, w_up, w_down, final_g, *[w for w, _ in cast_list])
    return res[0], list(res[1:])


SCAN_TB = 512
SCAN_NC = SCAN_TB // DN_CHUNK
SWA_SUB = SCAN_TB // WINDOW


def _swa_kernel(sink_ref, q_ref, k_ref, v_ref, kp_ref, vp_ref, o_ref):
    first = pl.program_id(1) == 0
    w = WINDOW
    grp = A_HEADS // A_KV_HEADS
    lane = lax.broadcasted_iota(jnp.int32, (w, LANES), 1)
    low = lane < A_HEAD_DIM
    row = lax.broadcasted_iota(jnp.int32, (w, w), 0)
    col = lax.broadcasted_iota(jnp.int32, (w, w), 1)
    cur_ok = jnp.concatenate([col <= row] * 2, axis=0)

    def dup_half(x, head):
        swapped = pltpu.roll(x, A_HEAD_DIM, 1)
        if head == 0:
            return jnp.where(low, x, swapped)
        return jnp.where(low, swapped, x)

    kvs = [(kp_ref[...], vp_ref[...])] + [(k_ref[sb * w:(sb + 1) * w, :], v_ref[sb * w:(sb + 1) * w, :])
                                           for sb in range(SWA_SUB)]
    dups = [[(dup_half(kk, kvh), dup_half(vv, kvh)) for kk, vv in kvs] for kvh in range(A_KV_HEADS)]
    k2s = [[jnp.concatenate([d[sb + 1][0], d[sb][0]], axis=0).astype(BF16) for sb in range(SWA_SUB)]
           for d in dups]
    v2s = [[jnp.concatenate([d[sb + 1][1], d[sb][1]], axis=0).astype(BF16) for sb in range(SWA_SUB)]
           for d in dups]
    scale = A_HEAD_DIM ** -0.5
    zero = jnp.zeros((), BF16)
    chains = [(sb, pr) for sb in range(SWA_SUB) for pr in range(A_HEADS // 2)]
    s2s, sinks = [], []
    for sb, pr in chains:
        pair = q_ref[sb * w:(sb + 1) * w, pr * LANES:(pr + 1) * LANES] * scale
        qst = jnp.concatenate([jnp.where(low, pair, 0.0), jnp.where(low, 0.0, pair)], axis=0)
        s2s.append(_dot_nt(qst.astype(BF16), k2s[2 * pr // grp][sb]))
        sinks.append(jnp.concatenate([jnp.full((w, w), sink_ref[2 * pr], F32),
                                      jnp.full((w, w), sink_ref[2 * pr + 1], F32)], axis=0))
    p2s = []
    for (sb, pr), s2, sink in zip(chains, s2s, sinks):
        prev = jnp.where(first, -jnp.inf, s2[:, w:]) if sb == 0 else s2[:, w:]
        s = jnp.where(cur_ok, s2[:, :w], prev)
        m = jnp.maximum(jnp.max(s, axis=-1, keepdims=True), sink)
        p = jnp.exp(s - m)
        denom = jnp.sum(p, axis=-1, keepdims=True) + jnp.exp(sink - m)
        pb = (p * (1.0 / denom)).astype(BF16)
        p2s.append(jnp.concatenate([jnp.where(cur_ok, pb, zero), jnp.where(cur_ok, zero, pb)], axis=-1))
    for (sb, pr), p2 in zip(chains, p2s):
        o = _dot(p2, v2s[2 * pr // grp][sb])
        o_ref[sb * w:(sb + 1) * w, pr * LANES:(pr + 1) * LANES] = (
            jnp.where(low, o[:w], o[w:]).astype(o_ref.dtype))


def _swa_specs():
    tq = SWA_SUB * WINDOW
    spec = lambda blk, prev: (
        pl.BlockSpec((None, WINDOW, LANES), lambda b, n: (b, jnp.maximum(n * SWA_SUB - 1, 0), blk))
        if prev else pl.BlockSpec((None, tq, LANES), lambda b, n: (b, n, blk)))
    return [pl.BlockSpec(memory_space=pltpu.SMEM),
            pl.BlockSpec((None, tq, A_Q), lambda b, n: (b, n, 0)),
            spec(BLK_KA, False), spec(BLK_VA, False), spec(BLK_KA, True), spec(BLK_VA, True)]


GATE_TB = 512
GDN_TB = 2048
GDN_NC = GDN_TB // DN_CHUNK
GDN_GRP = 128
GRP_NC = GDN_GRP // DN_CHUNK


def _split3(x):
    hi = x.astype(BF16)
    r1 = x - hi.astype(F32)
    mid = r1.astype(BF16)
    lo = (r1 - mid.astype(F32)).astype(BF16)
    return hi, mid, lo


def _gate_prep_kernel(gate_ref, alog_ref, dt_ref, o_ref, ot_ref):
    tb = GATE_TB
    gate = gate_ref[...]
    lane = lax.broadcasted_iota(jnp.int32, (tb, LANES), 1)
    beta = jax.nn.sigmoid(gate)
    g = -jnp.exp(alog_ref[...]) * jax.nn.softplus(gate + dt_ref[...])
    row = lax.broadcasted_iota(jnp.int32, (tb, tb), 0)
    col = lax.broadcasted_iota(jnp.int32, (tb, tb), 1)
    same = (row >> DN_CHUNK_LOG2) == (col >> DN_CHUNK_LOG2)
    tri = jnp.where(jnp.logical_and(same, row >= col), 1.0, 0.0).astype(BF16)
    ones = jnp.where(same, 1.0, 0.0).astype(BF16)
    r = _dot(jnp.concatenate([tri, ones], axis=0), jnp.concatenate(_split3(g), axis=-1))
    s = r[:, :LANES] + r[:, LANES:2 * LANES] + r[:, 2 * LANES:]
    out = jnp.where(lane < GATE_GAM, beta, jnp.where(lane < GATE_LAST, s[:tb], s[tb:]))
    o_ref[...] = out
    ot_ref[...] = out.T[:GATE_ROWS, :]


def _gate_prep(gates, alog_row, dt_row):
    n = gates.shape[0]
    tb = GATE_TB
    row = pl.BlockSpec((1, LANES), lambda i: (0, 0))
    return pl.pallas_call(
        _gate_prep_kernel,
        out_shape=(jax.ShapeDtypeStruct((n, LANES), F32),
                   jax.ShapeDtypeStruct((GATE_ROWS, n), F32)),
        grid=(n // tb,),
        in_specs=[pl.BlockSpec((tb, LANES), lambda i: (i, 0)), row, row],
        out_specs=(pl.BlockSpec((tb, LANES), lambda i: (i, 0)),
                   pl.BlockSpec((GATE_ROWS, tb), lambda i: (0, i))),
        compiler_params=_params(("parallel",), 8),
        name="gdn_gate_prep",
    )(gates, alog_row, dt_row)


def _conv_silu(x_ref, tail, w, buf_ref):
    t = x_ref.shape[0]
    x = x_ref[...]
    buf_ref[0:SUBLANES, :] = tail
    buf_ref[SUBLANES:, :] = x
    acc = x * w[CONV_K - 1:CONV_K, :]
    for j in range(CONV_K - 1):
        acc = acc + buf_ref[pl.ds(SUBLANES - (CONV_K - 1 - j), t), :] * w[j:j + 1, :]
    return _silu(acc)


def _lane_col(x, idx):
    lane = lax.broadcasted_iota(jnp.int32, x.shape, 1)
    return jnp.sum(jnp.where(lane == idx, x, 0.0), axis=-1, keepdims=True)


def _gdn_pre_kernel(q_ref, k_ref, v_ref, hq_ref, hk_ref, hv_ref, gc_ref, gr_ref,
                    cq_ref, ck_ref, cv_ref, *rest, n_cast, n_cast_step):
    cast_in = rest[:n_cast]
    kw_ref, nn_ref, qe_ref, ol_ref, gl_ref = rest[n_cast:n_cast + 5]
    cast_out = rest[n_cast + 5:2 * n_cast + 5]
    buf_ref = rest[2 * n_cast + 5]
    h = pl.program_id(1)
    first = pl.program_id(2) == 0
    gs = GDN_GRP

    _cast_tiles(cast_in[:n_cast_step], cast_out[:n_cast_step])

    @pl.when(first)
    def _():
        _cast_tiles(cast_in[n_cast_step:], cast_out[n_cast_step:])

    def conv(x_ref, halo_ref, w_ref, slot):
        tail = jnp.where(first, 0.0, halo_ref[...])
        return _conv_silu(x_ref, tail, w_ref[...], buf_ref.at[slot])

    q = conv(q_ref, hq_ref, cq_ref, 0)
    k = conv(k_ref, hk_ref, ck_ref, 1)
    v = conv(v_ref, hv_ref, cv_ref, 2)
    q = q * lax.rsqrt(jnp.sum(q * q, axis=-1, keepdims=True) + EPS) * (B_HEAD_DIM ** -0.5)
    k = k * lax.rsqrt(jnp.sum(k * k, axis=-1, keepdims=True) + EPS)

    gall = gc_ref[...]
    beta = _lane_col(gall, h + GATE_BETA)
    gam = _lane_col(gall, h + GATE_GAM)
    gam_end = _lane_col(gall, h + GATE_LAST)
    gam_row = gr_ref[pl.ds(h + GATE_GAM, 1), :]
    egam = jnp.exp(gam)
    q_dec = q * egam
    k_dec = k * jnp.exp(gam_end - gam)
    rhs_all = jnp.concatenate([v * beta, k * (beta * egam)], axis=-1)
    ends = gc_ref[pl.ds(DN_CHUNK - 1, GDN_NC, stride=DN_CHUNK), :]
    gl_ref[...] = jnp.broadcast_to(jnp.exp(_lane_col(ends, h + GATE_GAM)), (GDN_NC, LANES))

    row = lax.broadcasted_iota(jnp.int32, (gs, gs), 0)
    col = lax.broadcasted_iota(jnp.int32, (gs, gs), 1)
    same = (row >> DN_CHUNK_LOG2) == (col >> DN_CHUNK_LOG2)
    incl = jnp.logical_and(same, row >= col)
    strict = jnp.logical_and(same, row > col)
    trow = lax.broadcasted_iota(jnp.int32, (GRP_NC * B_HEAD_DIM, gs), 0)
    tcol = lax.broadcasted_iota(jnp.int32, (GRP_NC * B_HEAD_DIM, gs), 1)
    own = (trow >> 7) == (tcol >> DN_CHUNK_LOG2)

    groups = range(GDN_TB // gs)
    sls = [slice(grp * gs, (grp + 1) * gs) for grp in groups]
    kbs = [k[sl].astype(BF16) for sl in sls]
    scs = [_dot_nt(jnp.concatenate([q[sl].astype(BF16), kb], axis=0), kb)
           for sl, kb in zip(sls, kbs)]
    decays = [jnp.exp(jnp.where(incl, gam[sl] - gam_row[:, sl], -jnp.inf)) for sl in sls]
    qks = [(sc[:gs] * dec).astype(BF16) for sc, dec in zip(scs, decays)]
    xs = [jnp.where(strict, (-beta[sl]) * sc[gs:] * dec, 0.0)
          for sl, sc, dec in zip(sls, scs, decays)]
    ys = [rhs_all[sl] for sl in sls]
    for _ in range(DN_CHUNK_LOG2 - 1):
        xbs = [x.astype(BF16) for x in xs]
        rs = [_dot(xb, jnp.concatenate([xb, y.astype(BF16)], axis=-1))
              for xb, y in zip(xbs, ys)]
        xs = [r[:, :gs] for r in rs]
        ys = [y + r[:, gs:] for y, r in zip(ys, rs)]
    ys = [y + _dot(x.astype(BF16), y.astype(BF16)) for x, y in zip(xs, ys)]
    ybs = [y.astype(BF16) for y in ys]
    es = [_dot(qk, yb) for qk, yb in zip(qks, ybs)]
    kdt4s = [jnp.where(own, jnp.concatenate([k_dec[sl].T] * GRP_NC, axis=0), 0.0).astype(BF16)
             for sl in sls]
    fs = [_dot(kdt4, yb) for kdt4, yb in zip(kdt4s, ybs)]
    for grp, sl, e, f in zip(groups, sls, es, fs):
        ol_ref[sl, :] = e[:, :B_HEAD_DIM]
        qe_ref[sl, :] = (q_dec[sl] - e[:, B_HEAD_DIM:]).astype(qe_ref.dtype)
        rws = slice(grp * GRP_NC * B_HEAD_DIM, (grp + 1) * GRP_NC * B_HEAD_DIM)
        nn_ref[rws, :] = f[:, :B_HEAD_DIM]
        kw_ref[rws, :] = f[:, B_HEAD_DIM:].astype(kw_ref.dtype)


def _gdn_scan_kernel(kw_ref, nn_ref, qe_ref, ol_ref, gl_ref, *rest):
    nz = B_W // Z_SPLIT
    z_refs = rest[:nz]
    onorm_ref, o_ref, state_ref = rest[nz:]
    c = DN_CHUNK
    dk = B_HEAD_DIM
    hpz = Z_SPLIT // dk

    @pl.when(pl.program_id(1) == 0)
    def _():
        state_ref[...] = jnp.zeros_like(state_ref)

    onorm = onorm_ref[...]
    for n in range(SCAN_NC):
        ts = slice(n * c, (n + 1) * c)
        ds = slice(n * dk, (n + 1) * dk)
        for h in range(B_HEADS):
            hs = slice(h * dk, (h + 1) * dk)
            s = state_ref[h]
            r = _dot(jnp.concatenate([kw_ref[h, ds, :], qe_ref[h, ts, :]], axis=0), s.astype(BF16))
            state_ref[h] = s * gl_ref[h, n:n + 1, :] + nn_ref[h, ds, :] - r[:dk]
            o = ol_ref[h, ts, :] + r[dk:]
            o = o * lax.rsqrt(jnp.mean(o * o, axis=-1, keepdims=True) + EPS) * onorm
            z = z_refs[h // hpz][ts, (h % hpz) * dk:(h % hpz + 1) * dk]
            o_ref[ts, hs] = (o * _silu(z)).astype(o_ref.dtype)


def _gdn_pre(proj, gates, conv_w, alog_row, dt_row, batch, seq, cast_list):
    gcol, grow = _gate_prep(gates, alog_row, dt_row)
    tb = GDN_TB
    nt = seq // tb
    bf16_rows = 2 * SUBLANES
    n_steps = batch * B_HEADS * nt
    per_step = lambda wl: wl[0].shape[1] % (n_steps * bf16_rows) == 0
    order = sorted(range(len(cast_list)), key=lambda idx: not per_step(cast_list[idx]))
    cast_list = [cast_list[idx] for idx in order]
    n_cast_step = sum(per_step(wl) for wl in cast_list)
    cast = [_cast_specs(w, layer, n_steps, 1, lambda b, h, t: ((b * B_HEADS + h) * nt + t, 0))
            if per_step((w, layer)) else
            _cast_specs(w, layer, batch * B_HEADS, 1, lambda b, h, t: (b * B_HEADS + h, 0))
            for w, layer in cast_list]
    cast_in_specs, cast_out_specs, cast_shapes = ([c[k] for c in cast] for k in range(3))
    proj3 = proj.reshape(batch, seq, proj.shape[-1])
    gcol3 = gcol.reshape(batch, seq, LANES)
    nch = seq // DN_CHUNK
    dk = B_HEAD_DIM
    tok = lambda blk: pl.BlockSpec((None, tb, LANES), lambda b, h, t: (b, t, blk + h))
    halo = lambda blk: pl.BlockSpec(
        (None, SUBLANES, LANES),
        lambda b, h, t: (b, jnp.maximum(t * (tb // SUBLANES) - 1, 0), blk + h))
    cw = lambda blk: pl.BlockSpec((CONV_K, LANES), lambda b, h, t: (0, blk + h))
    per_chunk = lambda rows: pl.BlockSpec((None, None, rows, LANES), lambda b, h, t: (b, h, t, 0))
    kw, nn, qe, ol, gl, *casts = pl.pallas_call(
        functools.partial(_gdn_pre_kernel, n_cast=len(cast_list), n_cast_step=n_cast_step),
        out_shape=(jax.ShapeDtypeStruct((batch, B_HEADS, nch * dk, dk), BF16),
                   jax.ShapeDtypeStruct((batch, B_HEADS, nch * dk, dk), F32),
                   jax.ShapeDtypeStruct((batch, B_HEADS, seq, dk), BF16),
                   jax.ShapeDtypeStruct((batch, B_HEADS, seq, dk), F32),
                   jax.ShapeDtypeStruct((batch, B_HEADS, nch, LANES), F32),
                   *cast_shapes),
        grid=(batch, B_HEADS, nt),
        in_specs=[tok(BLK_QB), tok(BLK_KB), tok(BLK_VB),
                  halo(BLK_QB), halo(BLK_KB), halo(BLK_VB),
                  pl.BlockSpec((None, tb, LANES), lambda b, h, t: (b, t, 0)),
                  pl.BlockSpec((GATE_ROWS, tb), lambda b, h, t: (0, b * nt + t)),
                  cw(0), cw(B_HEADS), cw(2 * B_HEADS), *cast_in_specs],
        out_specs=(per_chunk(GDN_NC * dk), per_chunk(GDN_NC * dk), per_chunk(tb), per_chunk(tb),
                   per_chunk(GDN_NC), *cast_out_specs),
        scratch_shapes=[pltpu.VMEM((3, SUBLANES + tb, LANES), F32)],
        compiler_params=_params(("parallel", "parallel", "arbitrary"), 46),
        name="gdn_pre",
    )(proj3, proj3, proj3, proj3, proj3, proj3, gcol3, grow, conv_w, conv_w, conv_w,
      *[w for w, _ in cast_list])

    return (kw, nn, qe, ol, gl), [casts[order.index(idx)] for idx in range(len(order))]


def _mixers_tail_kernel(sink_ref, q_ref, k_ref, v_ref, kp_ref, vp_ref, *rest):
    *scan_in, oa_ref, ob_ref, state_ref = rest
    _gdn_scan_kernel(*scan_in, ob_ref, state_ref)
    _swa_kernel(sink_ref, q_ref, k_ref, v_ref, kp_ref, vp_ref, oa_ref)


def _mixers_tail(proj, sinks, pre, onorm, batch, seq):
    kw, nn, qe, ol, gl = pre
    proj3 = proj.reshape(batch, seq, proj.shape[-1])
    dk = B_HEAD_DIM
    heads = lambda rows: pl.BlockSpec((None, B_HEADS, rows, LANES), lambda b, t: (b, 0, t, 0))
    ts = SCAN_TB
    nz = B_W // Z_SPLIT
    assert (BLK_Z * LANES) % Z_SPLIT == 0
    z_blk0 = BLK_Z * LANES // Z_SPLIT
    out_a, out_b = pl.pallas_call(
        _mixers_tail_kernel,
        out_shape=(jax.ShapeDtypeStruct((batch, seq, A_Q), BF16),
                   jax.ShapeDtypeStruct((batch, seq, B_W), BF16)),
        grid=(batch, seq // ts),
        in_specs=_swa_specs()
        + [heads(SCAN_NC * dk), heads(SCAN_NC * dk), heads(ts), heads(ts), heads(SCAN_NC)]
        + [pl.BlockSpec((None, ts, Z_SPLIT), lambda b, t, k=k: (b, t, z_blk0 + k)) for k in range(nz)]
        + [pl.BlockSpec((1, dk), lambda b, t: (0, 0))],
        out_specs=(pl.BlockSpec((None, ts, A_Q), lambda b, t: (b, t, 0)),
                   pl.BlockSpec((None, ts, B_W), lambda b, t: (b, t, 0))),
        scratch_shapes=[pltpu.VMEM((B_HEADS, dk, dk), F32)],
        compiler_params=_params(("parallel", "arbitrary"), 48),
        name="mixers_tail",
    )(sinks, *([proj3] * 5), kw, nn, qe, ol, gl, *([proj3] * nz), onorm)
    return out_a.reshape(batch * seq, A_Q), out_b.reshape(batch * seq, B_W)


ODD_TM = 512
ODD_TN = 1024
ODD_NV = D_MODEL // ODD_TN


def _odd_mixer_kernel(x_ref, g_ref, win_ref, lg_ref, lb_ref, ws_ref, bs_ref, wout_ref,
                      o_ref, hn_ref, v_ref, stat_ref, gated_ref):
    j = pl.program_id(1)
    cc = C_CHUNK
    gpb = ODD_TN // C_GROUP_DIM
    d = ODD_NV * ODD_TN

    @pl.when(j == 0)
    def _():
        hn_ref[...] = _rms_rows(x_ref[...], g_ref[...]).astype(BF16)

    @pl.when(j < ODD_NV)
    def _():
        v_ref[j] = _gelu(_dot(hn_ref[...], win_ref[...]))

    for blk in range(ODD_NV):
        @pl.when(j == ODD_NV + blk)
        def _(blk=blk):
            if blk == 0:
                vs = [v_ref[k] for k in range(ODD_NV)]
                mu = sum(jnp.sum(vk, axis=-1, keepdims=True) for vk in vs) * (1.0 / d)
                var = sum(jnp.sum((vk - mu) * (vk - mu), axis=-1, keepdims=True) for vk in vs) * (1.0 / d)
                stat_ref[0] = mu
                stat_ref[1] = lax.rsqrt(var + EPS)
            vcols = slice(blk * ODD_TN, (blk + 1) * ODD_TN)
            y = ((v_ref[blk] - stat_ref[0]) * stat_ref[1] * lg_ref[:, vcols] + lb_ref[:, vcols]).astype(BF16)
            u = _gelu(_dot(hn_ref[...], win_ref[...]))
            row = lax.broadcasted_iota(jnp.int32, (cc, cc), 0)
            col = lax.broadcasted_iota(jnp.int32, (cc, cc), 1)
            bs = bs_ref[...]
            for gl in range(gpb):
                gi = blk * gpb + gl
                ws = jnp.where(row >= col, ws_ref[gi], 0.0).astype(BF16)
                bias = bs[:, gi:gi + 1]
                cs = slice(gi * C_GROUP_DIM, (gi + 1) * C_GROUP_DIM)
                us = slice(gl * C_GROUP_DIM, (gl + 1) * C_GROUP_DIM)
                for n in range(ODD_TM // cc):
                    rs = slice(n * cc, (n + 1) * cc)
                    mixed = _dot(ws, y[rs, us]) + bias
                    gated_ref[rs, cs] = (u[rs, us] * mixed).astype(BF16)

    @pl.when(j == 2 * ODD_NV)
    def _():
        o_ref[...] = x_ref[...] + _dot(gated_ref[...], wout_ref[...])


def _odd_mixer(x, g, w_in3, ln_g, ln_b, w_s, b_s_t, w_out3):
    n, d = x.shape
    tm, tn, nv = ODD_TM, ODD_TN, ODD_NV
    win_blk = lambda i, j: (0, 0, jnp.where(j < nv, j + nv, jnp.minimum(j - nv, nv - 1)))
    const = lambda *shape: pl.BlockSpec(shape, lambda i, j: (0,) * len(shape))
    return pl.pallas_call(
        _odd_mixer_kernel,
        out_shape=jax.ShapeDtypeStruct((n, d), F32),
        grid=(n // tm, 2 * nv + 1),
        in_specs=[pl.BlockSpec((tm, d), lambda i, j: (i, 0)),
                  const(1, d),
                  pl.BlockSpec((None, d, tn), win_blk),
                  const(1, d), const(1, d),
                  const(C_GROUPS, C_CHUNK, C_CHUNK),
                  const(C_CHUNK, C_GROUPS),
                  pl.BlockSpec((None, d, d), lambda i, j: (0, 0, 0))],
        out_specs=pl.BlockSpec((tm, d), lambda i, j: (i, 0)),
        scratch_shapes=[pltpu.VMEM((tm, d), BF16),
                        pltpu.VMEM((nv, tm, tn), F32),
                        pltpu.VMEM((2, tm, 1), F32),
                        pltpu.VMEM((tm, d), BF16)],
        compiler_params=_params(("parallel", "arbitrary"), 56),
        name="odd_mixer",
    )(x, g, w_in3, ln_g, ln_b, w_s, b_s_t, w_out3)


def kernel(x, even_norm, even_w_in, even_conv, even_a_log, even_dt_bias, even_sinks,
           even_onorm, even_w_out, odd_norm, odd_w_in, odd_ln_g, odd_ln_b, odd_w_s,
           odd_b_s, odd_w_out, ffn_norm, ffn_w_gate, ffn_w_up, ffn_w_down, final_norm):
    batch, seq, d = x.shape
    n = batch * seq
    h = x.reshape(n, d)
    row = lambda a: a.reshape(1, -1).astype(F32)

    w_bg = even_w_in[0, :, EVEN_MAIN:]
    w_gate = jnp.pad(jnp.concatenate([w_bg, w_bg[:, B_HEADS:]], axis=1),
                     ((0, 0), (0, LANES - GATE_ROWS))).astype(BF16)
    proj, gates = _norm_proj_even(h, row(even_norm[0]), even_w_in[0].astype(BF16), w_gate)
    def gate_row(a):
        a = a.astype(F32)
        return jnp.pad(jnp.concatenate([a, a]), (GATE_GAM, LANES - GATE_ROWS)).reshape(1, LANES)

    pre, (w_eo, wg0, wu0, wd0) = _gdn_pre(
        proj, gates, even_conv[0], gate_row(even_a_log[0]), gate_row(even_dt_bias[0]), batch, seq,
        [(even_w_out, 0), (ffn_w_gate, 0), (ffn_w_up, 0), (ffn_w_down, 0)])
    out_a, out_b = _mixers_tail(proj, even_sinks[0].astype(F32), pre, row(even_onorm[0]), batch, seq)
    h, (w_oi, w_oo) = _proj_residual([out_a, out_b], w_eo[None], h, [(odd_w_in, 0), (odd_w_out, 0)])
    h, (wg1, wu1, wd1) = _ffn(h, row(ffn_norm[0]), wg0, wu0, wd0, row(final_norm),
                              [(ffn_w_gate, 1), (ffn_w_up, 1), (ffn_w_down, 1)], final_norm=False)

    h = _odd_mixer(h, row(odd_norm[0]), w_oi[None], row(odd_ln_g[0]), row(odd_ln_b[0]),
                   odd_w_s[0], odd_b_s[0].T, w_oo[None])
    h, _ = _ffn(h, row(ffn_norm[1]), wg1, wu1, wd1, row(final_norm), [], final_norm=True)
    return h.reshape(batch, seq, d)
```

```python
import functools

import jax
import jax.numpy as jnp
from jax import lax
from jax.experimental import pallas as pl
from jax.experimental.pallas import tpu as pltpu

F32 = jnp.float32
BF16 = jnp.bfloat16

D_MODEL = 2048
A_HEADS = 16
A_KV_HEADS = 2
A_HEAD_DIM = 64
WINDOW = 128
B_HEADS = 8
B_HEAD_DIM = 128
CONV_K = 4
DN_CHUNK = 64
DN_CHUNK_LOG2 = 6
C_GROUPS = 8
C_CHUNK = 128
C_GROUP_DIM = D_MODEL // C_GROUPS
EPS = 1e-6

LANES = 128
SUBLANES = 8
A_Q = A_HEADS * A_HEAD_DIM
A_KV = A_KV_HEADS * A_HEAD_DIM
B_W = B_HEADS * B_HEAD_DIM
EVEN_MAIN = A_Q + 2 * A_KV + 4 * B_W
BLK_KA = A_Q // LANES
BLK_VA = BLK_KA + A_KV // LANES
BLK_QB = BLK_VA + A_KV // LANES
BLK_KB = BLK_QB + B_HEADS
BLK_VB = BLK_KB + B_HEADS
BLK_Z = BLK_VB + B_HEADS
Z_SPLIT = 2 * LANES
GATE_BETA = 0
GATE_GAM = B_HEADS
GATE_LAST = 2 * B_HEADS
GATE_ROWS = 3 * B_HEADS

MIB = 1024 * 1024


def _params(sem, vmem_mib):
    return pltpu.CompilerParams(dimension_semantics=sem, vmem_limit_bytes=vmem_mib * MIB)


def _cast_specs(w, layer, nr, nc, tile_of):
    rows, cols = w.shape[1] // nr, w.shape[2] // nc
    assert w.shape[1] % nr == 0 and w.shape[2] % nc == 0
    assert rows % (2 * SUBLANES) == 0 and cols % LANES == 0
    return (pl.BlockSpec((None, rows, cols), lambda *g: (layer, *tile_of(*g))),
            pl.BlockSpec((rows, cols), lambda *g: tuple(tile_of(*g))),
            jax.ShapeDtypeStruct(w.shape[1:], BF16))


def _cast_tiles(srcs, dsts):
    for src, dst in zip(srcs, dsts):
        dst[...] = src[...].astype(dst.dtype)


def _dot(a, b):
    return jnp.dot(a, b, preferred_element_type=F32)


def _dot_nt(a, b):
    return lax.dot_general(a, b, (((1,), (1,)), ((), ())), preferred_element_type=F32)


def _rms_rows(x, g):
    ms = jnp.mean(x * x, axis=-1, keepdims=True)
    return x * lax.rsqrt(ms + EPS) * g


def _silu(x):
    return x * jax.nn.sigmoid(x)


def _gelu(x):
    return x * (lax.erf(x * (2.0 ** -0.5)) + 1.0) * 0.5


def _norm_proj_kernel(x_ref, g_ref, w_ref, wg_ref, o_ref, og_ref, hn_ref):
    @pl.when(pl.program_id(1) == 0)
    def _():
        hn = _rms_rows(x_ref[...], g_ref[...]).astype(BF16)
        hn_ref[...] = hn
        og_ref[...] = _dot(hn, wg_ref[...])

    o_ref[...] = _dot(hn_ref[...], w_ref[...])


def _norm_proj_even(x, g, w_in, w_gate, *, tm=1024, tn=1792):
    n, d = x.shape
    nout = EVEN_MAIN
    assert nout % tn == 0
    return pl.pallas_call(
        _norm_proj_kernel,
        out_shape=(jax.ShapeDtypeStruct((n, nout), F32),
                   jax.ShapeDtypeStruct((n, LANES), F32)),
        grid=(n // tm, nout // tn),
        in_specs=[pl.BlockSpec((tm, d), lambda i, j: (i, 0)),
                  pl.BlockSpec((1, d), lambda i, j: (0, 0)),
                  pl.BlockSpec((d, tn), lambda i, j: (0, j)),
                  pl.BlockSpec((d, LANES), lambda i, j: (0, 0))],
        out_specs=(pl.BlockSpec((tm, tn), lambda i, j: (i, j)),
                   pl.BlockSpec((tm, LANES), lambda i, j: (i, 0))),
        scratch_shapes=[pltpu.VMEM((tm, d), BF16)],
        compiler_params=_params(("parallel", "arbitrary"), 56),
        name="norm_proj_even",
    )(x, g, w_in, w_gate)


def _proj_res_kernel(*refs, n_in, n_cast):
    a_refs = refs[:n_in]
    w_refs = refs[n_in:2 * n_in]
    res_ref = refs[2 * n_in]
    cast_in = refs[2 * n_in + 1:2 * n_in + 1 + n_cast]
    o_ref = refs[2 * n_in + 1 + n_cast]
    cast_out = refs[2 * n_in + 2 + n_cast:]
    _cast_tiles(cast_in, cast_out)
    acc = res_ref[...]
    for a_ref, w_ref in zip(a_refs, w_refs):
        acc = acc + _dot(a_ref[...], w_ref[...])
    o_ref[...] = acc


def _proj_residual(a_list, w3, res, cast_list, *, tm=512):
    n, d = res.shape
    n_in = len(a_list)
    kk = a_list[0].shape[1]
    assert all(a.shape[1] == kk for a in a_list) and w3.shape[1] == n_in * kk
    ni = n // tm
    cast = [_cast_specs(w, layer, ni, 1, lambda i: (i, 0)) for w, layer in cast_list]
    in_specs = ([pl.BlockSpec((tm, kk), lambda i: (i, 0)) for _ in a_list]
                + [pl.BlockSpec((None, kk, d), lambda i, k=k: (0, k, 0)) for k in range(n_in)]
                + [pl.BlockSpec((tm, d), lambda i: (i, 0))] + [c[0] for c in cast])
    res_all = pl.pallas_call(
        functools.partial(_proj_res_kernel, n_in=n_in, n_cast=len(cast)),
        out_shape=(jax.ShapeDtypeStruct((n, d), F32), *[c[2] for c in cast]),
        grid=(ni,),
        in_specs=in_specs,
        out_specs=(pl.BlockSpec((tm, d), lambda i: (i, 0)), *[c[1] for c in cast]),
        compiler_params=_params(("parallel",), 40),
        name="proj_residual",
    )(*a_list, *([w3] * n_in), res, *[w for w, _ in cast_list])
    return res_all[0], list(res_all[1:])


def _ffn_kernel(x_ref, g_ref, wg_ref, wu_ref, wd_ref, fg_ref, *rest, final_norm, n_cast):
    cast_in = rest[:n_cast]
    o_ref = rest[n_cast]
    cast_out = rest[n_cast + 1:2 * n_cast + 1]
    hn_ref = rest[2 * n_cast + 1]
    f = pl.program_id(1)
    _cast_tiles(cast_in, cast_out)

    @pl.when(f == 0)
    def _():
        x = x_ref[...]
        hn_ref[...] = _rms_rows(x, g_ref[...]).astype(BF16)
        o_ref[...] = x

    hn = hn_ref[...]
    a = (_silu(_dot(hn, wg_ref[...])) * _dot(hn, wu_ref[...])).astype(BF16)
    o_ref[...] += _dot(a, wd_ref[...])

    if final_norm:
        @pl.when(f == pl.num_programs(1) - 1)
        def _():
            o_ref[...] = _rms_rows(o_ref[...], fg_ref[...])


def _ffn(x, g, w_gate, w_up, w_down, final_g, cast_list, *, final_norm, tm=1024, tf=512):
    n, d = x.shape
    dff = w_gate.shape[1]
    ni, nf = n // tm, dff // tf
    cast = [_cast_specs(w, layer, ni, nf, lambda i, f: (i, f)) if w.shape[1] == d else
            _cast_specs(w, layer, nf, ni, lambda i, f: (f, i)) for w, layer in cast_list]
    res = pl.pallas_call(
        functools.partial(_ffn_kernel, final_norm=final_norm, n_cast=len(cast)),
        out_shape=(jax.ShapeDtypeStruct((n, d), F32), *[c[2] for c in cast]),
        grid=(ni, nf),
        in_specs=[pl.BlockSpec((tm, d), lambda i, f: (i, 0)),
                  pl.BlockSpec((1, d), lambda i, f: (0, 0)),
                  pl.BlockSpec((d, tf), lambda i, f: (0, f)),
                  pl.BlockSpec((d, tf), lambda i, f: (0, f)),
                  pl.BlockSpec((tf, d), lambda i, f: (f, 0)),
                  pl.BlockSpec((1, d), lambda i, f: (0, 0)), *[c[0] for c in cast]],
        out_specs=(pl.BlockSpec((tm, d), lambda i, f: (i, 0)), *[c[1] for c in cast]),
        scratch_shapes=[pltpu.VMEM((tm, d), BF16)],
        compiler_params=_params(("parallel", "arbitrary"), 58 if cast else 56),
        name="swiglu_ffn",
    )(x, g, w_gate, w_up, w_down, final_g, *[w for w, _ in cast_list])
    return res[0], list(res[1:])


SCAN_TB = 512
SCAN_NC = SCAN_TB // DN_CHUNK
SWA_SUB = SCAN_TB // WINDOW


def _swa_kernel(sink_ref, q_ref, k_ref, v_ref, kp_ref, vp_ref, o_ref):
    first = pl.program_id(1) == 0
    w = WINDOW
    grp = A_HEADS // A_KV_HEADS
    lane = lax.broadcasted_iota(jnp.int32, (w, LANES), 1)
    low = lane < A_HEAD_DIM
    row = lax.broadcasted_iota(jnp.int32, (w, w), 0)
    col = lax.broadcasted_iota(jnp.int32, (w, w), 1)
    cur_ok = jnp.concatenate([col <= row] * 2, axis=0)

    def dup_half(x, head):
        swapped = pltpu.roll(x, A_HEAD_DIM, 1)
        if head == 0:
            return jnp.where(low, x, swapped)
        return jnp.where(low, swapped, x)

    kvs = [(kp_ref[...], vp_ref[...])] + [(k_ref[sb * w:(sb + 1) * w, :], v_ref[sb * w:(sb + 1) * w, :])
                                           for sb in range(SWA_SUB)]
    dups = [[(dup_half(kk, kvh), dup_half(vv, kvh)) for kk, vv in kvs] for kvh in range(A_KV_HEADS)]
    k2s = [[jnp.concatenate([d[sb + 1][0], d[sb][0]], axis=0).astype(BF16) for sb in range(SWA_SUB)]
           for d in dups]
    v2s = [[jnp.concatenate([d[sb + 1][1], d[sb][1]], axis=0).astype(BF16) for sb in range(SWA_SUB)]
           for d in dups]
    scale = A_HEAD_DIM ** -0.5
    zero = jnp.zeros((), BF16)
    chains = [(sb, pr) for sb in range(SWA_SUB) for pr in range(A_HEADS // 2)]
    s2s, sinks = [], []
    for sb, pr in chains:
        pair = q_ref[sb * w:(sb + 1) * w, pr * LANES:(pr + 1) * LANES] * scale
        qst = jnp.concatenate([jnp.where(low, pair, 0.0), jnp.where(low, 0.0, pair)], axis=0)
        s2s.append(_dot_nt(qst.astype(BF16), k2s[2 * pr // grp][sb]))
        sinks.append(jnp.concatenate([jnp.full((w, w), sink_ref[2 * pr], F32),
                                      jnp.full((w, w), sink_ref[2 * pr + 1], F32)], axis=0))
    p2s = []
    for (sb, pr), s2, sink in zip(chains, s2s, sinks):
        prev = jnp.where(first, -jnp.inf, s2[:, w:]) if sb == 0 else s2[:, w:]
        s = jnp.where(cur_ok, s2[:, :w], prev)
        m = jnp.maximum(jnp.max(s, axis=-1, keepdims=True), sink)
        p = jnp.exp(s - m)
        denom = jnp.sum(p, axis=-1, keepdims=True) + jnp.exp(sink - m)
        pb = (p * (1.0 / denom)).astype(BF16)
        p2s.append(jnp.concatenate([jnp.where(cur_ok, pb, zero), jnp.where(cur_ok, zero, pb)], axis=-1))
    for (sb, pr), p2 in zip(chains, p2s):
        o = _dot(p2, v2s[2 * pr // grp][sb])
        o_ref[sb * w:(sb + 1) * w, pr * LANES:(pr + 1) * LANES] = (
            jnp.where(low, o[:w], o[w:]).astype(o_ref.dtype))


def _swa_specs():
    tq = SWA_SUB * WINDOW
    spec = lambda blk, prev: (
        pl.BlockSpec((None, WINDOW, LANES), lambda b, n: (b, jnp.maximum(n * SWA_SUB - 1, 0), blk))
        if prev else pl.BlockSpec((None, tq, LANES), lambda b, n: (b, n, blk)))
    return [pl.BlockSpec(memory_space=pltpu.SMEM),
            pl.BlockSpec((None, tq, A_Q), lambda b, n: (b, n, 0)),
            spec(BLK_KA, False), spec(BLK_VA, False), spec(BLK_KA, True), spec(BLK_VA, True)]


GATE_TB = 512
GDN_TB = 2048
GDN_NC = GDN_TB // DN_CHUNK
GDN_GRP = 128
GRP_NC = GDN_GRP // DN_CHUNK


def _split3(x):
    hi = x.astype(BF16)
    r1 = x - hi.astype(F32)
    mid = r1.astype(BF16)
    lo = (r1 - mid.astype(F32)).astype(BF16)
    return hi, mid, lo


def _gate_prep_kernel(gate_ref, alog_ref, dt_ref, o_ref, ot_ref):
    tb = GATE_TB
    gate = gate_ref[...]
    lane = lax.broadcasted_iota(jnp.int32, (tb, LANES), 1)
    beta = jax.nn.sigmoid(gate)
    g = -jnp.exp(alog_ref[...]) * jax.nn.softplus(gate + dt_ref[...])
    row = lax.broadcasted_iota(jnp.int32, (tb, tb), 0)
    col = lax.broadcasted_iota(jnp.int32, (tb, tb), 1)
    same = (row >> DN_CHUNK_LOG2) == (col >> DN_CHUNK_LOG2)
    tri = jnp.where(jnp.logical_and(same, row >= col), 1.0, 0.0).astype(BF16)
    ones = jnp.where(same, 1.0, 0.0).astype(BF16)
    r = _dot(jnp.concatenate([tri, ones], axis=0), jnp.concatenate(_split3(g), axis=-1))
    s = r[:, :LANES] + r[:, LANES:2 * LANES] + r[:, 2 * LANES:]
    out = jnp.where(lane < GATE_GAM, beta, jnp.where(lane < GATE_LAST, s[:tb], s[tb:]))
    o_ref[...] = out
    ot_ref[...] = out.T[:GATE_ROWS, :]


def _gate_prep(gates, alog_row, dt_row):
    n = gates.shape[0]
    tb = GATE_TB
    row = pl.BlockSpec((1, LANES), lambda i: (0, 0))
    return pl.pallas_call(
        _gate_prep_kernel,
        out_shape=(jax.ShapeDtypeStruct((n, LANES), F32),
                   jax.ShapeDtypeStruct((GATE_ROWS, n), F32)),
        grid=(n // tb,),
        in_specs=[pl.BlockSpec((tb, LANES), lambda i: (i, 0)), row, row],
        out_specs=(pl.BlockSpec((tb, LANES), lambda i: (i, 0)),
                   pl.BlockSpec((GATE_ROWS, tb), lambda i: (0, i))),
        compiler_params=_params(("parallel",), 8),
        name="gdn_gate_prep",
    )(gates, alog_row, dt_row)


def _conv_silu(x_ref, tail, w, buf_ref):
    t = x_ref.shape[0]
    x = x_ref[...]
    buf_ref[0:SUBLANES, :] = tail
    buf_ref[SUBLANES:, :] = x
    acc = x * w[CONV_K - 1:CONV_K, :]
    for j in range(CONV_K - 1):
        acc = acc + buf_ref[pl.ds(SUBLANES - (CONV_K - 1 - j), t), :] * w[j:j + 1, :]
    return _silu(acc)


def _lane_col(x, idx):
    lane = lax.broadcasted_iota(jnp.int32, x.shape, 1)
    return jnp.sum(jnp.where(lane == idx, x, 0.0), axis=-1, keepdims=True)


def _gdn_pre_kernel(q_ref, k_ref, v_ref, hq_ref, hk_ref, hv_ref, gc_ref, gr_ref,
                    cq_ref, ck_ref, cv_ref, *rest, n_cast, n_cast_step):
    cast_in = rest[:n_cast]
    kw_ref, nn_ref, qe_ref, ol_ref, gl_ref = rest[n_cast:n_cast + 5]
    cast_out = rest[n_cast + 5:2 * n_cast + 5]
    buf_ref = rest[2 * n_cast + 5]
    h = pl.program_id(1)
    first = pl.program_id(2) == 0
    gs = GDN_GRP

    _cast_tiles(cast_in[:n_cast_step], cast_out[:n_cast_step])

    @pl.when(first)
    def _():
        _cast_tiles(cast_in[n_cast_step:], cast_out[n_cast_step:])

    def conv(x_ref, halo_ref, w_ref, slot):
        tail = jnp.where(first, 0.0, halo_ref[...])
        return _conv_silu(x_ref, tail, w_ref[...], buf_ref.at[slot])

    q = conv(q_ref, hq_ref, cq_ref, 0)
    k = conv(k_ref, hk_ref, ck_ref, 1)
    v = conv(v_ref, hv_ref, cv_ref, 2)
    q = q * lax.rsqrt(jnp.sum(q * q, axis=-1, keepdims=True) + EPS) * (B_HEAD_DIM ** -0.5)
    k = k * lax.rsqrt(jnp.sum(k * k, axis=-1, keepdims=True) + EPS)

    gall = gc_ref[...]
    beta = _lane_col(gall, h + GATE_BETA)
    gam = _lane_col(gall, h + GATE_GAM)
    gam_end = _lane_col(gall, h + GATE_LAST)
    gam_row = gr_ref[pl.ds(h + GATE_GAM, 1), :]
    egam = jnp.exp(gam)
    q_dec = q * egam
    k_dec = k * jnp.exp(gam_end - gam)
    rhs_all = jnp.concatenate([v * beta, k * (beta * egam)], axis=-1)
    ends = gc_ref[pl.ds(DN_CHUNK - 1, GDN_NC, stride=DN_CHUNK), :]
    gl_ref[...] = jnp.broadcast_to(jnp.exp(_lane_col(ends, h + GATE_GAM)), (GDN_NC, LANES))

    row = lax.broadcasted_iota(jnp.int32, (gs, gs), 0)
    col = lax.broadcasted_iota(jnp.int32, (gs, gs), 1)
    same = (row >> DN_CHUNK_LOG2) == (col >> DN_CHUNK_LOG2)
    incl = jnp.logical_and(same, row >= col)
    strict = jnp.logical_and(same, row > col)
    trow = lax.broadcasted_iota(jnp.int32, (GRP_NC * B_HEAD_DIM, gs), 0)
    tcol = lax.broadcasted_iota(jnp.int32, (GRP_NC * B_HEAD_DIM, gs), 1)
    own = (trow >> 7) == (tcol >> DN_CHUNK_LOG2)

    groups = range(GDN_TB // gs)
    sls = [slice(grp * gs, (grp + 1) * gs) for grp in groups]
    kbs = [k[sl].astype(BF16) for sl in sls]
    scs = [_dot_nt(jnp.concatenate([q[sl].astype(BF16), kb], axis=0), kb)
           for sl, kb in zip(sls, kbs)]
    decays = [jnp.exp(jnp.where(incl, gam[sl] - gam_row[:, sl], -jnp.inf)) for sl in sls]
    qks = [(sc[:gs] * dec).astype(BF16) for sc, dec in zip(scs, decays)]
    xs = [jnp.where(strict, (-beta[sl]) * sc[gs:] * dec, 0.0)
          for sl, sc, dec in zip(sls, scs, decays)]
    ys = [rhs_all[sl] for sl in sls]
    for _ in range(DN_CHUNK_LOG2 - 1):
        xbs = [x.astype(BF16) for x in xs]
        rs = [_dot(xb, jnp.concatenate([xb, y.astype(BF16)], axis=-1))
              for xb, y in zip(xbs, ys)]
        xs = [r[:, :gs] for r in rs]
        ys = [y + r[:, gs:] for y, r in zip(ys, rs)]
    ys = [y + _dot(x.astype(BF16), y.astype(BF16)) for x, y in zip(xs, ys)]
    ybs = [y.astype(BF16) for y in ys]
    es = [_dot(qk, yb) for qk, yb in zip(qks, ybs)]
    kdt4s = [jnp.where(own, jnp.concatenate([k_dec[sl].T] * GRP_NC, axis=0), 0.0).astype(BF16)
             for sl in sls]
    fs = [_dot(kdt4, yb) for kdt4, yb in zip(kdt4s, ybs)]
    for grp, sl, e, f in zip(groups, sls, es, fs):
        ol_ref[sl, :] = e[:, :B_HEAD_DIM]
        qe_ref[sl, :] = (q_dec[sl] - e[:, B_HEAD_DIM:]).astype(qe_ref.dtype)
        rws = slice(grp * GRP_NC * B_HEAD_DIM, (grp + 1) * GRP_NC * B_HEAD_DIM)
        nn_ref[rws, :] = f[:, :B_HEAD_DIM]
        kw_ref[rws, :] = f[:, B_HEAD_DIM:].astype(kw_ref.dtype)


def _gdn_scan_kernel(kw_ref, nn_ref, qe_ref, ol_ref, gl_ref, *rest):
    nz = B_W // Z_SPLIT
    z_refs = rest[:nz]
    onorm_ref, o_ref, state_ref = rest[nz:]
    c = DN_CHUNK
    dk = B_HEAD_DIM
    hpz = Z_SPLIT // dk

    @pl.when(pl.program_id(1) == 0)
    def _():
        state_ref[...] = jnp.zeros_like(state_ref)

    onorm = onorm_ref[...]
    for n in range(SCAN_NC):
        ts = slice(n * c, (n + 1) * c)
        ds = slice(n * dk, (n + 1) * dk)
        for h in range(B_HEADS):
            hs = slice(h * dk, (h + 1) * dk)
            s = state_ref[h]
            r = _dot(jnp.concatenate([kw_ref[h, ds, :], qe_ref[h, ts, :]], axis=0), s.astype(BF16))
            state_ref[h] = s * gl_ref[h, n:n + 1, :] + nn_ref[h, ds, :] - r[:dk]
            o = ol_ref[h, ts, :] + r[dk:]
            o = o * lax.rsqrt(jnp.mean(o * o, axis=-1, keepdims=True) + EPS) * onorm
            z = z_refs[h // hpz][ts, (h % hpz) * dk:(h % hpz + 1) * dk]
            o_ref[ts, hs] = (o * _silu(z)).astype(o_ref.dtype)


def _gdn_pre(proj, gates, conv_w, alog_row, dt_row, batch, seq, cast_list):
    gcol, grow = _gate_prep(gates, alog_row, dt_row)
    tb = GDN_TB
    nt = seq // tb
    bf16_rows = 2 * SUBLANES
    n_steps = batch * B_HEADS * nt
    per_step = lambda wl: wl[0].shape[1] % (n_steps * bf16_rows) == 0
    order = sorted(range(len(cast_list)), key=lambda idx: not per_step(cast_list[idx]))
    cast_list = [cast_list[idx] for idx in order]
    n_cast_step = sum(per_step(wl) for wl in cast_list)
    cast = [_cast_specs(w, layer, n_steps, 1, lambda b, h, t: ((b * B_HEADS + h) * nt + t, 0))
            if per_step((w, layer)) else
            _cast_specs(w, layer, batch * B_HEADS, 1, lambda b, h, t: (b * B_HEADS + h, 0))
            for w, layer in cast_list]
    cast_in_specs, cast_out_specs, cast_shapes = ([c[k] for c in cast] for k in range(3))
    proj3 = proj.reshape(batch, seq, proj.shape[-1])
    gcol3 = gcol.reshape(batch, seq, LANES)
    nch = seq // DN_CHUNK
    dk = B_HEAD_DIM
    tok = lambda blk: pl.BlockSpec((None, tb, LANES), lambda b, h, t: (b, t, blk + h))
    halo = lambda blk: pl.BlockSpec(
        (None, SUBLANES, LANES),
        lambda b, h, t: (b, jnp.maximum(t * (tb // SUBLANES) - 1, 0), blk + h))
    cw = lambda blk: pl.BlockSpec((CONV_K, LANES), lambda b, h, t: (0, blk + h))
    per_chunk = lambda rows: pl.BlockSpec((None, None, rows, LANES), lambda b, h, t: (b, h, t, 0))
    kw, nn, qe, ol, gl, *casts = pl.pallas_call(
        functools.partial(_gdn_pre_kernel, n_cast=len(cast_list), n_cast_step=n_cast_step),
        out_shape=(jax.ShapeDtypeStruct((batch, B_HEADS, nch * dk, dk), BF16),
                   jax.ShapeDtypeStruct((batch, B_HEADS, nch * dk, dk), F32),
                   jax.ShapeDtypeStruct((batch, B_HEADS, seq, dk), BF16),
                   jax.ShapeDtypeStruct((batch, B_HEADS, seq, dk), F32),
                   jax.ShapeDtypeStruct((batch, B_HEADS, nch, LANES), F32),
                   *cast_shapes),
        grid=(batch, B_HEADS, nt),
        in_specs=[tok(BLK_QB), tok(BLK_KB), tok(BLK_VB),
                  halo(BLK_QB), halo(BLK_KB), halo(BLK_VB),
                  pl.BlockSpec((None, tb, LANES), lambda b, h, t: (b, t, 0)),
                  pl.BlockSpec((GATE_ROWS, tb), lambda b, h, t: (0, b * nt + t)),
                  cw(0), cw(B_HEADS), cw(2 * B_HEADS), *cast_in_specs],
        out_specs=(per_chunk(GDN_NC * dk), per_chunk(GDN_NC * dk), per_chunk(tb), per_chunk(tb),
                   per_chunk(GDN_NC), *cast_out_specs),
        scratch_shapes=[pltpu.VMEM((3, SUBLANES + tb, LANES), F32)],
        compiler_params=_params(("parallel", "parallel", "arbitrary"), 46),
        name="gdn_pre",
    )(proj3, proj3, proj3, proj3, proj3, proj3, gcol3, grow, conv_w, conv_w, conv_w,
      *[w for w, _ in cast_list])

    return (kw, nn, qe, ol, gl), [casts[order.index(idx)] for idx in range(len(order))]


def _mixers_tail_kernel(sink_ref, q_ref, k_ref, v_ref, kp_ref, vp_ref, *rest):
    *scan_in, oa_ref, ob_ref, state_ref = rest
    _gdn_scan_kernel(*scan_in, ob_ref, state_ref)
    _swa_kernel(sink_ref, q_ref, k_ref, v_ref, kp_ref, vp_ref, oa_ref)


def _mixers_tail(proj, sinks, pre, onorm, batch, seq):
    kw, nn, qe, ol, gl = pre
    proj3 = proj.reshape(batch, seq, proj.shape[-1])
    dk = B_HEAD_DIM
    heads = lambda rows: pl.BlockSpec((None, B_HEADS, rows, LANES), lambda b, t: (b, 0, t, 0))
    ts = SCAN_TB
    nz = B_W // Z_SPLIT
    assert (BLK_Z * LANES) % Z_SPLIT == 0
    z_blk0 = BLK_Z * LANES // Z_SPLIT
    out_a, out_b = pl.pallas_call(
        _mixers_tail_kernel,
        out_shape=(jax.ShapeDtypeStruct((batch, seq, A_Q), BF16),
                   jax.ShapeDtypeStruct((batch, seq, B_W), BF16)),
        grid=(batch, seq // ts),
        in_specs=_swa_specs()
        + [heads(SCAN_NC * dk), heads(SCAN_NC * dk), heads(ts), heads(ts), heads(SCAN_NC)]
        + [pl.BlockSpec((None, ts, Z_SPLIT), lambda b, t, k=k: (b, t, z_blk0 + k)) for k in range(nz)]
        + [pl.BlockSpec((1, dk), lambda b, t: (0, 0))],
        out_specs=(pl.BlockSpec((None, ts, A_Q), lambda b, t: (b, t, 0)),
                   pl.BlockSpec((None, ts, B_W), lambda b, t: (b, t, 0))),
        scratch_shapes=[pltpu.VMEM((B_HEADS, dk, dk), F32)],
        compiler_params=_params(("parallel", "arbitrary"), 48),
        name="mixers_tail",
    )(sinks, *([proj3] * 5), kw, nn, qe, ol, gl, *([proj3] * nz), onorm)
    return out_a.reshape(batch * seq, A_Q), out_b.reshape(batch * seq, B_W)


ODD_TM = 512
ODD_TN = 1024
ODD_NV = D_MODEL // ODD_TN


def _odd_mixer_kernel(x_ref, g_ref, win_ref, lg_ref, lb_ref, ws_ref, bs_ref, wout_ref,
                      o_ref, hn_ref, v_ref, stat_ref, gated_ref):
    j = pl.program_id(1)
    cc = C_CHUNK
    gpb = ODD_TN // C_GROUP_DIM
    d = ODD_NV * ODD_TN

    @pl.when(j == 0)
    def _():
        hn_ref[...] = _rms_rows(x_ref[...], g_ref[...]).astype(BF16)

    @pl.when(j < ODD_NV)
    def _():
        v_ref[j] = _gelu(_dot(hn_ref[...], win_ref[...]))

    for blk in range(ODD_NV):
        @pl.when(j == ODD_NV + blk)
        def _(blk=blk):
            if blk == 0:
                vs = [v_ref[k] for k in range(ODD_NV)]
                mu = sum(jnp.sum(vk, axis=-1, keepdims=True) for vk in vs) * (1.0 / d)
                var = sum(jnp.sum((vk - mu) * (vk - mu), axis=-1, keepdims=True) for vk in vs) * (1.0 / d)
                stat_ref[0] = mu
                stat_ref[1] = lax.rsqrt(var + EPS)
            vcols = slice(blk * ODD_TN, (blk + 1) * ODD_TN)
            y = ((v_ref[blk] - stat_ref[0]) * stat_ref[1] * lg_ref[:, vcols] + lb_ref[:, vcols]).astype(BF16)
            u = _gelu(_dot(hn_ref[...], win_ref[...]))
            row = lax.broadcasted_iota(jnp.int32, (cc, cc), 0)
            col = lax.broadcasted_iota(jnp.int32, (cc, cc), 1)
            bs = bs_ref[...]
            for gl in range(gpb):
                gi = blk * gpb + gl
                ws = jnp.where(row >= col, ws_ref[gi], 0.0).astype(BF16)
                bias = bs[:, gi:gi + 1]
                cs = slice(gi * C_GROUP_DIM, (gi + 1) * C_GROUP_DIM)
                us = slice(gl * C_GROUP_DIM, (gl + 1) * C_GROUP_DIM)
                for n in range(ODD_TM // cc):
                    rs = slice(n * cc, (n + 1) * cc)
                    mixed = _dot(ws, y[rs, us]) + bias
                    gated_ref[rs, cs] = (u[rs, us] * mixed).astype(BF16)
            if blk == ODD_NV - 1:
                o_ref[...] = x_ref[...] + _dot(gated_ref[...], wout_ref[...])


def _odd_mixer(x, g, w_in3, ln_g, ln_b, w_s, b_s_t, w_out3):
    n, d = x.shape
    tm, tn, nv = ODD_TM, ODD_TN, ODD_NV
    win_blk = lambda i, j: (0, 0, jnp.where(j < nv, j + nv, j - nv))
    const = lambda *shape: pl.BlockSpec(shape, lambda i, j: (0,) * len(shape))
    return pl.pallas_call(
        _odd_mixer_kernel,
        out_shape=jax.ShapeDtypeStruct((n, d), F32),
        grid=(n // tm, 2 * nv),
        in_specs=[pl.BlockSpec((tm, d), lambda i, j: (i, 0)),
                  const(1, d),
                  pl.BlockSpec((None, d, tn), win_blk),
                  const(1, d), const(1, d),
                  const(C_GROUPS, C_CHUNK, C_CHUNK),
                  const(C_CHUNK, C_GROUPS),
                  pl.BlockSpec((None, d, d), lambda i, j: (0, 0, 0))],
        out_specs=pl.BlockSpec((tm, d), lambda i, j: (i, 0)),
        scratch_shapes=[pltpu.VMEM((tm, d), BF16),
                        pltpu.VMEM((nv, tm, tn), F32),
                        pltpu.VMEM((2, tm, 1), F32),
                        pltpu.VMEM((tm, d), BF16)],
        compiler_params=_params(("parallel", "arbitrary"), 56),
        name="odd_mixer",
    )(x, g, w_in3, ln_g, ln_b, w_s, b_s_t, w_out3)


def kernel(x, even_norm, even_w_in, even_conv, even_a_log, even_dt_bias, even_sinks,
           even_onorm, even_w_out, odd_norm, odd_w_in, odd_ln_g, odd_ln_b, odd_w_s,
           odd_b_s, odd_w_out, ffn_norm, ffn_w_gate, ffn_w_up, ffn_w_down, final_norm):
    batch, seq, d = x.shape
    n = batch * seq
    h = x.reshape(n, d)
    row = lambda a: a.reshape(1, -1).astype(F32)

    w_bg = even_w_in[0, :, EVEN_MAIN:]
    w_gate = jnp.pad(jnp.concatenate([w_bg, w_bg[:, B_HEADS:]], axis=1),
                     ((0, 0), (0, LANES - GATE_ROWS))).astype(BF16)
    proj, gates = _norm_proj_even(h, row(even_norm[0]), even_w_in[0].astype(BF16), w_gate)
    def gate_row(a):
        a = a.astype(F32)
        return jnp.pad(jnp.concatenate([a, a]), (GATE_GAM, LANES - GATE_ROWS)).reshape(1, LANES)

    pre, (w_eo, wg0, wu0, wd0) = _gdn_pre(
        proj, gates, even_conv[0], gate_row(even_a_log[0]), gate_row(even_dt_bias[0]), batch, seq,
        [(even_w_out, 0), (ffn_w_gate, 0), (ffn_w_up, 0), (ffn_w_down, 0)])
    out_a, out_b = _mixers_tail(proj, even_sinks[0].astype(F32), pre, row(even_onorm[0]), batch, seq)
    h, (w_oi, w_oo) = _proj_residual([out_a, out_b], w_eo[None], h, [(odd_w_in, 0), (odd_w_out, 0)])
    h, (wg1, wu1, wd1) = _ffn(h, row(ffn_norm[0]), wg0, wu0, wd0, row(final_norm),
                              [(ffn_w_gate, 1), (ffn_w_up, 1), (ffn_w_down, 1)], final_norm=False)

    h = _odd_mixer(h, row(odd_norm[0]), w_oi[None], row(odd_ln_g[0]), row(odd_ln_b[0]),
                   odd_w_s[0], odd_b_s[0].T, w_oo[None])
    h, _ = _ffn(h, row(ffn_norm[1]), wg1, wu1, wd1, row(final_norm), [], final_norm=True)
    return h.reshape(batch, seq, d)
```
